```python
import math, functools
import jax, jax.numpy as jnp
from jax import lax
import numpy as np

D_MODEL = 1024
BATCH = 8
SEQ = 2048
DEPTH = 1
DEC_BATCH = 128
DEC_SEQ = 4
PAST_LEN = 8192
PAGE_SIZE = 128

D_HEAD = 64
SB_HEADS = 8
NSA_HEADS = 8
NSA_KV_HEADS = 2
NSA_REP = NSA_HEADS // NSA_KV_HEADS
SB_WIDTH = SB_HEADS * D_HEAD
NSA_WIDTH = NSA_HEADS * D_HEAD
KV_WIDTH = 2 * NSA_KV_HEADS * D_HEAD
N_NSA_BRANCHES = 3
CMP_BLOCK = 32
SEL_BLOCK = 64
TOP_N = 16
WINDOW = 512
Q_BLOCK = 128
SEL_Q_BLOCK = 32
D_FF = 4 * D_MODEL
IN_WIDTH = 3 * SB_WIDTH + NSA_WIDTH + 3 * KV_WIDTH + N_NSA_BRANCHES * NSA_HEADS + 2 * D_MODEL
RMS_EPS = 1e-6
FORCE_SCORE = 1e4
ATTN_SCALE = 1.0 / math.sqrt(D_HEAD)

kernel_name = 'stickbreak_nsa_hybrid_decode_step'


def _rmsnorm(x, g):
    xf = x.astype(jnp.float32)
    y = xf * lax.rsqrt(jnp.mean(xf * xf, axis=-1, keepdims=True) + RMS_EPS)
    return (y * g.astype(jnp.float32)).astype(x.dtype)


def _alibi_slopes(n):
    return jnp.exp2(-8.0 * (jnp.arange(n, dtype=jnp.float32) + 1.0) / n)


def _masked_softmax(s, mask, axis):
    s = jnp.where(mask, s, -jnp.inf)
    m = jnp.max(s, axis=axis, keepdims=True)
    m = jnp.where(jnp.isfinite(m), m, 0.0)
    e = jnp.where(mask, jnp.exp(s - m), 0.0)
    return e / jnp.maximum(jnp.sum(e, axis=axis, keepdims=True), 1e-30)


def _adaln(c, w_ada, b_ada):
    mod = jax.nn.silu(c) @ w_ada + b_ada
    return jnp.split(mod[:, None, :], 6, axis=-1)


def _split_projection(z):
    n, t = z.shape[:2]
    sizes = (SB_WIDTH, 2 * SB_WIDTH, NSA_WIDTH, KV_WIDTH, KV_WIDTH, KV_WIDTH,
             N_NSA_BRANCHES * NSA_HEADS, D_MODEL, D_MODEL)
    cuts = [int(v) for v in np.cumsum(sizes)[:-1]]
    sb_q, sb_kv, nsa_q, cmp_kv, sel_kv, win_kv, nsa_g, g_sb, g_nsa = jnp.split(z, cuts, axis=-1)
    kv_shape = (n, t, 2, NSA_KV_HEADS, D_HEAD)
    return dict(
        sb_q=sb_q.reshape(n, t, SB_HEADS, D_HEAD),
        sb_kv=sb_kv.reshape(n, t, 2, SB_HEADS, D_HEAD),
        nsa_q=nsa_q.reshape(n, t, NSA_KV_HEADS, NSA_REP, D_HEAD),
        cmp_kv=cmp_kv.reshape(kv_shape), sel_kv=sel_kv.reshape(kv_shape), win_kv=win_kv.reshape(kv_shape),
        nsa_gate=jax.nn.sigmoid(nsa_g.reshape(n, t, NSA_KV_HEADS, NSA_REP, N_NSA_BRANCHES)),
        gate_sb=jax.nn.sigmoid(g_sb), gate_nsa=jax.nn.sigmoid(g_nsa))


def _sb_block(q, k, v, mask, carry_log):
    z = jnp.einsum('nqhd,nshd->nhqs', q, k, preferred_element_type=jnp.float32) * ATTN_SCALE
    l1m = jnp.where(mask, jax.nn.log_sigmoid(-z), 0.0)
    after = lax.cumsum(l1m, axis=3, reverse=True) - l1m
    a = jnp.where(mask, jnp.exp(jax.nn.log_sigmoid(z) + after + carry_log[..., None]), 0.0)
    o = jnp.einsum('nhqs,nshd->nqhd', a, v.astype(jnp.float32))
    return o, carry_log + jnp.sum(l1m, axis=-1)


def _sb_prompt(q, kv):
    n, t = q.shape[:2]
    k, v = kv[:, :, 0], kv[:, :, 1]
    key_pos = jnp.arange(t)

    def one(i):
        qb = lax.dynamic_slice_in_dim(q, i * Q_BLOCK, Q_BLOCK, axis=1)
        q_pos = i * Q_BLOCK + jnp.arange(Q_BLOCK)
        o, _ = _sb_block(qb, k, v, key_pos[None, :] < q_pos[:, None],
                         jnp.zeros((n, SB_HEADS, Q_BLOCK), jnp.float32))
        return o

    o = lax.map(one, jnp.arange(t // Q_BLOCK))
    return jnp.moveaxis(o, 0, 1).reshape(n, t, SB_HEADS, D_HEAD)


def _sb_sample(q, kv_new, cache_sb_kv, layer, page_table):
    n, nq = q.shape[:2]
    tq = jnp.arange(nq)
    o, carry = _sb_block(q, kv_new[:, :, 0], kv_new[:, :, 1], tq[None, :] < tq[:, None],
                         jnp.zeros((n, SB_HEADS, nq), jnp.float32))

    def step(acc, pages):
        o_acc, c_acc = acc
        kv = cache_sb_kv[layer, pages]
        full = jnp.ones((nq, kv.shape[1]), bool)
        o_b, c_acc = _sb_block(q, kv[:, :, 0], kv[:, :, 1], full, c_acc)
        return (o_acc + o_b, c_acc), None

    (o, _), _ = lax.scan(step, (o, carry), page_table.T, reverse=True)
    return o


def _compress(kv, w):
    n, length = kv.shape[:2]
    blocks = kv.reshape(n, length // CMP_BLOCK, CMP_BLOCK, 2, NSA_KV_HEADS, D_HEAD)
    return jnp.einsum('nbicgd,icg->nbcgd', blocks, w)


def _cmp_attend(q, ckv, q_pos, slopes):
    nc = ckv.shape[1]
    end_pos = jnp.arange(nc) * CMP_BLOCK + CMP_BLOCK - 1
    s = jnp.einsum('nqgrd,ncgd->ngrqc', q, ckv[:, :, 0], preferred_element_type=jnp.float32) * ATTN_SCALE
    dist = (q_pos[:, None] - end_pos[None, :]).astype(jnp.float32)
    s = s - slopes[:, :, None, None] * dist
    p = _masked_softmax(s, end_pos[None, :] <= q_pos[:, None], axis=-1)
    o = jnp.einsum('ngrqc,ncgd->nqgrd', p, ckv[:, :, 1].astype(jnp.float32))
    return o, jnp.sum(p, axis=2)


def _select_blocks(p_cmp, q_pos, n_sel):
    n, g, nq, nc = p_cmp.shape
    per = SEL_BLOCK // CMP_BLOCK
    p = jnp.pad(p_cmp, ((0, 0), (0, 0), (0, 0), (0, n_sel * per - nc)))
    score = jnp.sum(p.reshape(n, g, nq, n_sel, per), axis=-1)
    j = jnp.arange(n_sel)[None, :]
    cur = (q_pos // SEL_BLOCK)[:, None]
    forced = (j == 0) | (j == cur) | (j == cur - 1)
    valid = j * SEL_BLOCK <= q_pos[:, None]
    score = jnp.where(forced, FORCE_SCORE, jnp.where(valid, score, -1.0))
    _, idx = lax.top_k(score, min(TOP_N, n_sel))
    return idx


def _sel_attend(q, kv, key_pos, q_pos, slopes):
    s = jnp.einsum('ngqrd,ngqksd->ngqrks', q, kv[..., 0, :], preferred_element_type=jnp.float32) * ATTN_SCALE
    tq = q_pos[None, None, :, None, None, None]
    kp = key_pos[:, :, :, None]
    s = s - slopes[None, :, None, :, None, None] * (tq - kp).astype(jnp.float32)
    p = _masked_softmax(s, kp <= tq, axis=(4, 5))
    return jnp.einsum('ngqrks,ngqksd->ngqrd', p, kv[..., 1, :].astype(jnp.float32))


def _win_attend(q, kv, key_pos, q_pos, slopes):
    s = jnp.einsum('nqgrd,nsgd->ngrqs', q, kv[:, :, 0], preferred_element_type=jnp.float32) * ATTN_SCALE
    dist = q_pos[:, None] - key_pos[None, :]
    s = s - slopes[:, :, None, None] * dist.astype(jnp.float32)
    mask = (dist >= 0) & (dist < WINDOW) & (key_pos[None, :] >= 0)
    p = _masked_softmax(s, mask, axis=-1)
    return jnp.einsum('ngrqs,nsgd->nqgrd', p, kv[:, :, 1].astype(jnp.float32))


def _nsa_merge(gate, o_cmp, o_sel, o_win):
    return gate[..., 0:1] * o_cmp + gate[..., 1:2] * o_sel + gate[..., 2:3] * o_win


def _nsa_prompt(q, cmp_kv, sel_kv, win_kv, gate, cmp_w, slopes):
    n, t = q.shape[:2]
    pos = jnp.arange(t)
    o_cmp, p_cmp = _cmp_attend(q, _compress(cmp_kv, cmp_w), pos, slopes)
    n_sel = t // SEL_BLOCK
    idx = _select_blocks(p_cmp, pos, n_sel)
    sel_blocks = sel_kv.reshape(n, n_sel, SEL_BLOCK, 2, NSA_KV_HEADS, D_HEAD)
    qg = q.transpose(0, 2, 1, 3, 4)
    n_idx = jnp.arange(n)[:, None, None, None]
    g_idx = jnp.arange(NSA_KV_HEADS)[None, :, None, None]
    offs = jnp.arange(SEL_BLOCK)

    def one_sel(i):
        qb = lax.dynamic_slice_in_dim(qg, i * SEL_Q_BLOCK, SEL_Q_BLOCK, axis=2)
        ib = lax.dynamic_slice_in_dim(idx, i * SEL_Q_BLOCK, SEL_Q_BLOCK, axis=2)
        kv = sel_blocks[n_idx, ib, :, :, g_idx]
        key_pos = ib[..., None] * SEL_BLOCK + offs
        return _sel_attend(qb, kv, key_pos, i * SEL_Q_BLOCK + jnp.arange(SEL_Q_BLOCK), slopes)

    o_sel = lax.map(one_sel, jnp.arange(t // SEL_Q_BLOCK))
    o_sel = o_sel.transpose(1, 0, 3, 2, 4, 5).reshape(n, t, NSA_KV_HEADS, NSA_REP, D_HEAD)

    padded = jnp.pad(win_kv, ((0, 0), (WINDOW, 0), (0, 0), (0, 0), (0, 0)))

    def one_win(i):
        start = i * Q_BLOCK
        qb = lax.dynamic_slice_in_dim(q, start, Q_BLOCK, axis=1)
        kvb = lax.dynamic_slice_in_dim(padded, start, WINDOW + Q_BLOCK, axis=1)
        key_pos = start - WINDOW + jnp.arange(WINDOW + Q_BLOCK)
        return _win_attend(qb, kvb, key_pos, start + jnp.arange(Q_BLOCK), slopes)

    o_win = lax.map(one_win, jnp.arange(t // Q_BLOCK))
    o_win = jnp.moveaxis(o_win, 0, 1).reshape(n, t, NSA_KV_HEADS, NSA_REP, D_HEAD)
    return _nsa_merge(gate, o_cmp, o_sel, o_win)


def _nsa_sample(q, cmp_kv, sel_kv, win_kv, gate, cache_cmp_kv, cache_sel_kv, win_buf,
                layer, page_table, cmp_w, slopes):
    n, nq = q.shape[:2]
    n_pages = page_table.shape[1]
    past = n_pages * PAGE_SIZE
    pos = past + jnp.arange(nq)
    past_c = lax.map(lambda pg: _compress(cache_cmp_kv[layer, pg], cmp_w), page_table.T)
    past_c = jnp.moveaxis(past_c, 0, 1).reshape(n, past // CMP_BLOCK, 2, NSA_KV_HEADS, D_HEAD)
    n_new_c = nq // CMP_BLOCK
    ckv = jnp.concatenate([past_c, _compress(cmp_kv[:, :n_new_c * CMP_BLOCK], cmp_w)], axis=1)
    o_cmp, p_cmp = _cmp_attend(q, ckv, pos, slopes)
    n_sel = -(-(past + nq) // SEL_BLOCK)
    idx = _select_blocks(p_cmp, pos, n_sel)
    bpp = PAGE_SIZE // SEL_BLOCK
    n_past_sel = past // SEL_BLOCK
    n_tail = -(-nq // SEL_BLOCK)
    tail = jnp.pad(sel_kv, ((0, 0), (0, n_tail * SEL_BLOCK - nq), (0, 0), (0, 0), (0, 0)))
    tail = tail.reshape(n, n_tail, SEL_BLOCK, 2, NSA_KV_HEADS, D_HEAD)
    qg = q.transpose(0, 2, 1, 3, 4)
    n_idx = jnp.arange(n)[:, None, None]
    g_idx = jnp.arange(NSA_KV_HEADS)[None, :, None]
    offs = jnp.arange(SEL_BLOCK)

    def one_sel(t):
        it = idx[:, :, t]
        jp = jnp.minimum(it, n_past_sel - 1)
        page = page_table[n_idx, jp // bpp][..., None]
        rows = (jp % bpp)[..., None] * SEL_BLOCK + offs
        kv_past = cache_sel_kv[layer, page, rows, :, g_idx[..., None]]
        jt = jnp.clip(it - n_past_sel, 0, n_tail - 1)
        kv_new = tail[n_idx, jt, :, :, g_idx]
        kv = jnp.where((it >= n_past_sel)[..., None, None, None], kv_new, kv_past)
        key_pos = it[..., None] * SEL_BLOCK + offs
        qt = lax.dynamic_slice_in_dim(qg, t, 1, axis=2)
        q_pos = jnp.reshape(past + t, (1,))
        return _sel_attend(qt, kv[:, :, None], key_pos[:, :, None], q_pos, slopes)[:, :, 0]

    o_sel = jnp.moveaxis(lax.map(one_sel, jnp.arange(nq)), 0, 1)
    w = win_buf.shape[1]
    kv_w = jnp.concatenate([win_buf, win_kv], axis=1)
    o_win = _win_attend(q, kv_w, past - w + jnp.arange(w + nq), pos, slopes)
    new_buf = kv_w[:, kv_w.shape[1] - min(WINDOW, w + nq):]
    return _nsa_merge(gate, o_cmp, o_sel, o_win), new_buf


def _prompt_mixers(p, cmp_w, slopes):
    t = p['sb_q'].shape[1]
    o_sb = _sb_prompt(p['sb_q'], p['sb_kv'])
    o_nsa = _nsa_prompt(p['nsa_q'], p['cmp_kv'], p['sel_kv'], p['win_kv'], p['nsa_gate'], cmp_w, slopes)
    states = (p['sb_kv'], p['cmp_kv'], p['sel_kv'], p['win_kv'][:, t - min(WINDOW, t):])
    return o_sb, o_nsa, states


def _sample_mixers(p, layer, cache_sb_kv, cache_cmp_kv, cache_sel_kv, state_win_kv, page_table, cmp_w, slopes):
    o_sb = _sb_sample(p['sb_q'], p['sb_kv'], cache_sb_kv, layer, page_table)
    o_nsa, new_buf = _nsa_sample(p['nsa_q'], p['cmp_kv'], p['sel_kv'], p['win_kv'], p['nsa_gate'],
                                 cache_cmp_kv, cache_sel_kv, state_win_kv[layer], layer, page_table,
                                 cmp_w, slopes)
    return o_sb, o_nsa, (p['sb_kv'], p['cmp_kv'], p['sel_kv'], new_buf)


def _decoder_layer(x, c, mix_fn, w_ada, b_ada, g_mix, g_mlp, w_in, w_br_sb, w_br_nsa, w_out, w_up, w_down):
    n, t = x.shape[:2]
    sh1, sc1, gt1, sh2, sc2, gt2 = _adaln(c, w_ada, b_ada)
    h = _rmsnorm(x, g_mix) * (1.0 + sc1) + sh1
    p = _split_projection(h @ w_in)
    o_sb, o_nsa, states = mix_fn(p)
    branch_sb = o_sb.reshape(n, t, SB_WIDTH).astype(x.dtype) @ w_br_sb
    branch_nsa = o_nsa.reshape(n, t, NSA_WIDTH).astype(x.dtype) @ w_br_nsa
    mixed = (p['gate_sb'] * branch_sb + p['gate_nsa'] * branch_nsa) @ w_out
    x = x + gt1 * mixed
    h = _rmsnorm(x, g_mlp) * (1.0 + sc2) + sh2
    u = jnp.square(jax.nn.relu(h @ w_up))
    x = x + gt2 * (u @ w_down)
    return x, states


def setup_inputs(seed: int = 0) -> dict:
    key = jax.random.key(seed)
    ks = jax.random.split(key, 24)
    f32 = jnp.float32
    n_pages = PAST_LEN // PAGE_SIZE
    n_used = DEC_BATCH * n_pages
    n_pool = n_used + max(1, n_used // 4)
    win_rows = min(WINDOW, PAST_LEN)

    def nrm(k, shape, s):
        return jax.random.normal(k, shape, f32) * s

    return {
        'x_prompt': jax.random.normal(ks[0], (BATCH, SEQ, D_MODEL), f32),
        'x_sample': jax.random.normal(ks[1], (DEC_BATCH, DEC_SEQ, D_MODEL), f32),
        'cache_sb_kv': jax.random.normal(ks[2], (DEPTH, n_pool, PAGE_SIZE, 2, SB_HEADS, D_HEAD), f32),
        'cache_cmp_kv': jax.random.normal(ks[3], (DEPTH, n_pool, PAGE_SIZE, 2, NSA_KV_HEADS, D_HEAD), f32),
        'cache_sel_kv': jax.random.normal(ks[4], (DEPTH, n_pool, PAGE_SIZE, 2, NSA_KV_HEADS, D_HEAD), f32),
        'state_win_kv': jax.random.normal(ks[5], (DEPTH, DEC_BATCH, win_rows, 2, NSA_KV_HEADS, D_HEAD), f32),
        'page_table': jax.random.permutation(ks[6], n_pool)[:n_used].reshape(DEC_BATCH, n_pages).astype(jnp.int32),
        'c_prompt': jax.random.normal(ks[7], (BATCH, D_MODEL), f32),
        'c_sample': jax.random.normal(ks[8], (DEC_BATCH, D_MODEL), f32),
        'w_ada': nrm(ks[9], (DEPTH, D_MODEL, 6 * D_MODEL), D_MODEL ** -0.5),
        'b_ada': nrm(ks[10], (DEPTH, 6 * D_MODEL), 0.02),
        'g_mix': 1.0 + nrm(ks[11], (DEPTH, D_MODEL), 0.05),
        'g_mlp': 1.0 + nrm(ks[12], (DEPTH, D_MODEL), 0.05),
        'g_final': 1.0 + nrm(ks[13], (D_MODEL,), 0.05),
        'w_in': nrm(ks[14], (DEPTH, D_MODEL, IN_WIDTH), D_MODEL ** -0.5),
        'cmp_w': (1.0 + nrm(ks[15], (DEPTH, CMP_BLOCK, 2, NSA_KV_HEADS), 0.1)) / CMP_BLOCK,
        'w_br_sb': nrm(ks[16], (DEPTH, SB_WIDTH, D_MODEL), SB_WIDTH ** -0.5),
        'w_br_nsa': nrm(ks[17], (DEPTH, NSA_WIDTH, D_MODEL), NSA_WIDTH ** -0.5),
        'w_out': nrm(ks[18], (DEPTH, D_MODEL, D_MODEL), D_MODEL ** -0.5),
        'w_up': nrm(ks[19], (DEPTH, D_MODEL, D_FF), D_MODEL ** -0.5),
        'w_down': nrm(ks[20], (DEPTH, D_FF, D_MODEL), D_FF ** -0.5),
    }


def reference(x_prompt, x_sample, cache_sb_kv, cache_cmp_kv, cache_sel_kv, state_win_kv, page_table,
              c_prompt, c_sample, w_ada, b_ada, g_mix, g_mlp, g_final, w_in, cmp_w,
              w_br_sb, w_br_nsa, w_out, w_up, w_down):
    slopes = _alibi_slopes(NSA_HEADS).reshape(NSA_KV_HEADS, NSA_REP)
    xp, xs = x_prompt, x_sample
    st_p = ([], [], [], [])
    st_s = ([], [], [], [])
    for l in range(DEPTH):
        lw = (w_ada[l], b_ada[l], g_mix[l], g_mlp[l], w_in[l], w_br_sb[l], w_br_nsa[l],
              w_out[l], w_up[l], w_down[l])
        mix_p = functools.partial(_prompt_mixers, cmp_w=cmp_w[l], slopes=slopes)
        mix_s = functools.partial(_sample_mixers, layer=l, cache_sb_kv=cache_sb_kv,
                                  cache_cmp_kv=cache_cmp_kv, cache_sel_kv=cache_sel_kv,
                                  state_win_kv=state_win_kv, page_table=page_table,
                                  cmp_w=cmp_w[l], slopes=slopes)
        xp, sp = _decoder_layer(xp, c_prompt, mix_p, *lw)
        xs, ss = _decoder_layer(xs, c_sample, mix_s, *lw)
        for acc, s in zip(st_p, sp):
            acc.append(s)
        for acc, s in zip(st_s, ss):
            acc.append(s)
    y_prompt = _rmsnorm(xp, g_final)
    y_sample = _rmsnorm(xs, g_final)
    new_sb_kv_prompt = jnp.stack(st_p[0])
    new_sb_kv_sample = jnp.stack(st_s[0])
    new_cmp_kv_prompt = jnp.stack(st_p[1])
    new_cmp_kv_sample = jnp.stack(st_s[1])
    new_sel_kv_prompt = jnp.stack(st_p[2])
    new_sel_kv_sample = jnp.stack(st_s[2])
    new_win_kv_prompt = jnp.stack(st_p[3])
    new_win_kv_sample = jnp.stack(st_s[3])
    return (y_prompt, y_sample, new_sb_kv_prompt, new_sb_kv_sample, new_cmp_kv_prompt, new_cmp_kv_sample,
            new_sel_kv_prompt, new_sel_kv_sample, new_win_kv_prompt, new_win_kv_sample)
```

```python
import functools
import math

import jax
import jax.numpy as jnp
from jax import lax
from jax.experimental import pallas as pl
from jax.experimental.pallas import tpu as pltpu

F32 = jnp.float32
BF16 = jnp.bfloat16

D_HEAD = 64
SB_HEADS = 8
NSA_HEADS = 8
NSA_KV_HEADS = 2
NSA_REP = NSA_HEADS // NSA_KV_HEADS
SB_WIDTH = SB_HEADS * D_HEAD
NSA_WIDTH = NSA_HEADS * D_HEAD
KV_WIDTH = 2 * NSA_KV_HEADS * D_HEAD
N_NSA_BRANCHES = 3
CMP_BLOCK = 32
SEL_BLOCK = 64
TOP_N = 16
WINDOW = 512
PAGE_SIZE = 128
RMS_EPS = 1e-6
FORCE_SCORE = 1e4
ATTN_SCALE = 0.125
LANES = 128
NEG = -1e30
VMEM_LIMIT = 56 * 1024 * 1024

_C_SBQ = 0
_C_SBKV = _C_SBQ + SB_WIDTH
_C_NSQ = _C_SBKV + 2 * SB_WIDTH
_C_CMP = _C_NSQ + NSA_WIDTH
_C_SEL = _C_CMP + KV_WIDTH
_C_WIN = _C_SEL + KV_WIDTH
_C_GATE = _C_WIN + KV_WIDTH
_C_GSB = _C_GATE + N_NSA_BRANCHES * NSA_HEADS


def _params(*sem):
    return pltpu.CompilerParams(dimension_semantics=sem, vmem_limit_bytes=VMEM_LIMIT)


def _nt(a, b):
    return lax.dot_general(a, b, (((1,), (1,)), ((), ())), preferred_element_type=F32)


def _mm(a, b):
    return jnp.dot(a, b, preferred_element_type=F32)


def _modulated_norm(x, g, sc, sh):
    y = x * lax.rsqrt(jnp.mean(x * x, axis=-1, keepdims=True) + RMS_EPS)
    return (y * g) * (1.0 + sc) + sh


def _adaln_kernel(c_ref, w_ref, b_ref, o_ref):
    c = c_ref[...]
    s = c * jax.nn.sigmoid(c)
    o_ref[...] = _mm(s.astype(BF16), w_ref[...].astype(BF16)) + b_ref[...]


def _adaln(c, w_ada, b_ada):
    n, d = c.shape
    n_out = w_ada.shape[1]
    tn = 512
    return pl.pallas_call(
        _adaln_kernel,
        grid=(n_out // tn,),
        in_specs=[pl.BlockSpec((n, d), lambda j: (0, 0)),
                  pl.BlockSpec((d, tn), lambda j: (0, j)),
                  pl.BlockSpec((1, tn), lambda j: (0, j))],
        out_specs=pl.BlockSpec((n, tn), lambda j: (0, j)),
        out_shape=jax.ShapeDtypeStruct((n, n_out), F32),
        compiler_params=_params("arbitrary"),
        name="adaln",
    )(c, w_ada, b_ada.reshape(1, n_out))


def _proj_kernel(x_ref, g_ref, sc_ref, sh_ref, w_ref, *out_refs, groups):
    hb = _modulated_norm(x_ref[...], g_ref[...], sc_ref[...], sh_ref[...]).astype(BF16)
    k = 0
    for start, width, act, dtypes in groups:
        z = _mm(hb, w_ref[:, start:start + width])
        if act == "sigmoid":
            z = jax.nn.sigmoid(z)
        for dt in dtypes:
            out_refs[k][...] = z.astype(dt)
            k += 1


def _mod_spec(mod, tm, rows_per_seq):
    d = mod.shape[-1]
    if mod.shape[1] == 1:
        tiles = rows_per_seq // tm
        return pl.BlockSpec((None, 1, d), lambda i: (i // tiles, 0, 0))
    return pl.BlockSpec((None, tm, d), lambda i: (i, 0, 0))


def _project(x, g, sc, sh, w_cat, groups, tm, rows_per_seq):
    m, d = x.shape
    out_shape, out_specs = [], []
    for _, width, _, dtypes in groups:
        for dt in dtypes:
            out_shape.append(jax.ShapeDtypeStruct((m, width), dt))
            out_specs.append(pl.BlockSpec((tm, width), lambda i: (i, 0)))
    return pl.pallas_call(
        functools.partial(_proj_kernel, groups=groups),
        grid=(m // tm,),
        in_specs=[pl.BlockSpec((tm, d), lambda i: (i, 0)),
                  pl.BlockSpec((1, d), lambda i: (0, 0)),
                  _mod_spec(sc, tm, rows_per_seq), _mod_spec(sh, tm, rows_per_seq),
                  pl.BlockSpec(w_cat.shape, lambda i: (0, 0))],
        out_specs=out_specs,
        out_shape=out_shape,
        compiler_params=_params("arbitrary"),
        name="in_proj",
    )(x, g, sc, sh, w_cat)


def _mix_kernel(x_ref, g_ref, sc_ref, sh_ref, gt_ref, osb_ref, onsa_ref, wg_ref, wbs_ref, wbn_ref, wo_ref, o_ref):
    x = x_ref[...]
    d = x.shape[1]
    hb = _modulated_norm(x, g_ref[...], sc_ref[...], sh_ref[...]).astype(BF16)
    gate_sb = jax.nn.sigmoid(_mm(hb, wg_ref[:, :d]))
    gate_nsa = jax.nn.sigmoid(_mm(hb, wg_ref[:, d:]))
    mixed = gate_sb * _mm(osb_ref[...], wbs_ref[...]) + gate_nsa * _mm(onsa_ref[...], wbn_ref[...])
    o_ref[...] = x + gt_ref[...] * _mm(mixed.astype(BF16), wo_ref[...])


def _mix(x, g, sc, sh, gt, o_sb, o_nsa, w_gate, w_br_sb, w_br_nsa, w_out, tm, rows_per_seq):
    m, d = x.shape
    full = lambda a: pl.BlockSpec(a.shape, lambda i: (0, 0))
    return pl.pallas_call(
        _mix_kernel,
        grid=(m // tm,),
        in_specs=[pl.BlockSpec((tm, d), lambda i: (i, 0)), pl.BlockSpec((1, d), lambda i: (0, 0)),
                  _mod_spec(sc, tm, rows_per_seq), _mod_spec(sh, tm, rows_per_seq), _mod_spec(gt, tm, rows_per_seq),
                  pl.BlockSpec((tm, o_sb.shape[1]), lambda i: (i, 0)),
                  pl.BlockSpec((tm, o_nsa.shape[1]), lambda i: (i, 0)),
                  full(w_gate), full(w_br_sb), full(w_br_nsa), full(w_out)],
        out_specs=pl.BlockSpec((tm, d), lambda i: (i, 0)),
        out_shape=jax.ShapeDtypeStruct((m, d), F32),
        compiler_params=_params("arbitrary"),
        name="mix_out",
    )(x, g, sc, sh, gt, o_sb, o_nsa, w_gate, w_br_sb, w_br_nsa, w_out)


def _mlp_kernel(x_ref, g_ref, sc_ref, sh_ref, gt_ref, gf_ref, wu_ref, wd_ref, o_ref, *, final):
    x = x_ref[...]
    hb = _modulated_norm(x, g_ref[...], sc_ref[...], sh_ref[...]).astype(BF16)
    u = jnp.square(jnp.maximum(_mm(hb, wu_ref[...]), 0.0))
    x2 = x + gt_ref[...] * _mm(u.astype(BF16), wd_ref[...])
    if final:
        x2 = x2 * lax.rsqrt(jnp.mean(x2 * x2, axis=-1, keepdims=True) + RMS_EPS) * gf_ref[...]
    o_ref[...] = x2


def _mlp(x, g, sc, sh, gt, g_final, w_up, w_down, tm, rows_per_seq, final):
    m, d = x.shape
    full = lambda a: pl.BlockSpec(a.shape, lambda i: (0, 0))
    return pl.pallas_call(
        functools.partial(_mlp_kernel, final=final),
        grid=(m // tm,),
        in_specs=[pl.BlockSpec((tm, d), lambda i: (i, 0)), pl.BlockSpec((1, d), lambda i: (0, 0)),
                  _mod_spec(sc, tm, rows_per_seq), _mod_spec(sh, tm, rows_per_seq), _mod_spec(gt, tm, rows_per_seq),
                  pl.BlockSpec((1, d), lambda i: (0, 0)), full(w_up), full(w_down)],
        out_specs=pl.BlockSpec((tm, d), lambda i: (i, 0)),
        out_shape=jax.ShapeDtypeStruct((m, d), F32),
        compiler_params=_params("arbitrary"),
        name="mlp_final",
    )(x, g, sc, sh, gt, g_final, w_up, w_down)


def _log_sigmoids(z):
    sp = jnp.log(1.0 + jnp.exp(-jnp.abs(z)))
    return jnp.minimum(z, 0.0) - sp, -jnp.maximum(z, 0.0) - sp


def _split_bf16(x):
    hi = x.astype(BF16)
    return hi, (x - hi.astype(F32)).astype(BF16)


def _split3_bf16(x):
    hi = x.astype(BF16)
    r1 = x - hi.astype(F32)
    mid = r1.astype(BF16)
    return hi, mid, (r1 - mid.astype(F32)).astype(BF16)


def _later_and_rowsum():
    r = lax.broadcasted_iota(jnp.int32, (2 * LANES, 2 * LANES), 0)
    c = lax.broadcasted_iota(jnp.int32, (2 * LANES, 2 * LANES), 1)
    return jnp.where((c >= LANES) | ((r & (LANES - 1)) > c), 1.0, 0.0).astype(BF16)


def _sb_block(z, msk, pv, later, acc_ref, carry_ref):
    lsz, l1m = _log_sigmoids(z)
    if msk is not None:
        l1m = jnp.where(msk, l1m, 0.0)
    hi, lo = _split_bf16(l1m)
    cs = _mm(jnp.concatenate([hi, lo], axis=1), later)
    a = jnp.exp(lsz + cs[:, :LANES] + carry_ref[...])
    if msk is not None:
        a = jnp.where(msk, a, 0.0)
    acc_ref[...] += pv(a.astype(BF16))
    carry_ref[...] += cs[:, LANES:]


def _sb_prompt_kernel(q_ref, k_ref, v_ref, o_ref, acc_ref, carry_ref, *, tq):
    i = pl.program_id(2)
    nb = tq // LANES
    lane = lax.broadcasted_iota(jnp.int32, (1, LANES), 1)
    q = q_ref[...]
    zero = jnp.zeros_like(q)
    qs = jnp.concatenate([jnp.where(lane < D_HEAD, q, zero), jnp.where(lane >= D_HEAD, q, zero)], axis=0)
    later = _later_and_rowsum()
    q_pos = i * tq + lax.rem(lax.broadcasted_iota(jnp.int32, (2 * tq, 1), 0), tq)
    acc_ref[...] = jnp.zeros_like(acc_ref)
    carry_ref[...] = jnp.zeros_like(carry_ref)

    def tile(j, masked):
        off = pl.multiple_of(j * LANES, LANES)
        kb = k_ref[pl.ds(off, LANES), :]
        vb = v_ref[pl.ds(off, LANES), :]
        msk = (j * LANES + lane) < q_pos if masked else None
        _sb_block(_nt(qs, kb), msk, lambda a: _mm(a, vb), later, acc_ref, carry_ref)

    for u in range(nb):
        tile(i * nb + (nb - 1 - u), True)

    def body(jj, c):
        tile(i * nb - 1 - jj, False)
        return c

    lax.fori_loop(0, i * nb, body, 0)
    acc = acc_ref[...]
    o_ref[...] = jnp.where(lane < D_HEAD, acc[:tq], acc[tq:]).astype(o_ref.dtype)


def _sb_prompt(q, kv, n, t, tq):
    pairs = SB_WIDTH // LANES
    nq = t // tq
    return pl.pallas_call(
        functools.partial(_sb_prompt_kernel, tq=tq),
        grid=(n, pairs, nq),
        in_specs=[pl.BlockSpec((tq, LANES), lambda b, p, i: (b * nq + i, p)),
                  pl.BlockSpec((t, LANES), lambda b, p, i: (b, p)),
                  pl.BlockSpec((t, LANES), lambda b, p, i: (b, pairs + p))],
        out_specs=pl.BlockSpec((tq, LANES), lambda b, p, i: (b * nq + i, p)),
        out_shape=jax.ShapeDtypeStruct((n * t, SB_WIDTH), BF16),
        scratch_shapes=[pltpu.VMEM((2 * tq, LANES), F32), pltpu.VMEM((2 * tq, LANES), F32)],
        compiler_params=_params("arbitrary", "arbitrary", "arbitrary"),
        name="sb_prompt",
    )(q, kv, kv)


def _flash_tile(s, msk, vb, m_ref, l_ref, acc_ref):
    s = jnp.where(msk, s, NEG)
    m_old = m_ref[...]
    m_new = jnp.maximum(m_old, jnp.max(s, axis=1, keepdims=True))
    p = jnp.where(msk, jnp.exp(s - m_new), 0.0)
    alpha = jnp.exp(m_old - m_new)
    l_ref[...] = alpha * l_ref[...] + jnp.sum(p, axis=1, keepdims=True)
    acc_ref[...] = alpha * acc_ref[...] + _mm(p.astype(BF16), vb)
    m_ref[...] = m_new


def _nsa_prompt_kernel(slopes_ref, q_ref, cmp_ref, cw_ref, ksel_ref, vsel_ref, kwin_ref, vwin_ref,
                       g0_ref, g1_ref, g2_ref, o_ref, ck_ref, m_ref, l_ref, acc_ref, osel_ref, *, tq, t_len):
    g = pl.program_id(1)
    i = pl.program_id(2)
    nsel = t_len // SEL_BLOCK
    rows = NSA_REP * tq
    lane = lax.broadcasted_iota(jnp.int32, (1, LANES), 1)
    low = lane < D_HEAD

    @pl.when(i == 0)
    def _compress():
        xr = cmp_ref[...].reshape(nsel, SEL_BLOCK, KV_WIDTH)
        cw = cw_ref[...][None]
        own = jnp.where(low, 0, 1) == g
        halves = (jnp.sum(xr[:, :CMP_BLOCK, :] * cw, axis=1), jnp.sum(xr[:, CMP_BLOCK:, :] * cw, axis=1))
        for e, c in enumerate(halves):
            for kv in range(2):
                part = c[:, kv * LANES:(kv + 1) * LANES]
                ck_ref[2 * kv + e] = jnp.where(own, part, pltpu.roll(part, D_HEAD, 1)).astype(BF16)

    q = q_ref[...]
    zero = jnp.zeros((tq, LANES), q.dtype)
    pieces = []
    for b in range(NSA_REP // 2):
        qb = q[:, b * LANES:(b + 1) * LANES]
        pieces += [jnp.where(low, qb, zero), jnp.where(low, zero, qb)]
    qs = jnp.concatenate(pieces, axis=0)
    slope = [slopes_ref[g * NSA_REP + r] for r in range(NSA_REP)]
    slope_col = jnp.concatenate([jnp.full((tq, 1), slope[r], F32) for r in range(NSA_REP)], axis=0)
    t_col = i * tq + lax.rem(lax.broadcasted_iota(jnp.int32, (rows, 1), 0), tq)

    kde, kdo, vde, vdo = ck_ref[0], ck_ref[1], ck_ref[2], ck_ref[3]

    jl = lax.broadcasted_iota(jnp.int32, (1, nsel), 1)
    end_e = jl * SEL_BLOCK + (CMP_BLOCK - 1)
    end_o = jl * SEL_BLOCK + (SEL_BLOCK - 1)
    s_e = _nt(qs, kde) - slope_col * (t_col - end_e).astype(F32)
    s_o = _nt(qs, kdo) - slope_col * (t_col - end_o).astype(F32)
    ok_e = end_e <= t_col
    ok_o = end_o <= t_col
    mx = jnp.maximum(jnp.max(jnp.where(ok_e, s_e, NEG), axis=1, keepdims=True),
                     jnp.max(jnp.where(ok_o, s_o, NEG), axis=1, keepdims=True))
    mx = jnp.where(mx > 0.5 * NEG, mx, 0.0)
    e_e = jnp.where(ok_e, jnp.exp(s_e - mx), 0.0)
    e_o = jnp.where(ok_o, jnp.exp(s_o - mx), 0.0)
    den = jnp.maximum(jnp.sum(e_e, axis=1, keepdims=True) + jnp.sum(e_o, axis=1, keepdims=True), 1e-30)
    o_cmp = _mm((e_e / den).astype(BF16), vde) + _mm((e_o / den).astype(BF16), vdo)

    jr = lax.broadcasted_iota(jnp.int32, (nsel, tq), 0)
    tl = i * tq + lax.broadcasted_iota(jnp.int32, (nsel, tq), 1)
    end_et = jr * SEL_BLOCK + (CMP_BLOCK - 1)
    end_ot = jr * SEL_BLOCK + (SEL_BLOCK - 1)
    ok_et = end_et <= tl
    ok_ot = end_ot <= tl
    st_e = _nt(kde, qs)
    st_o = _nt(kdo, qs)
    score = jnp.zeros((nsel, tq), F32)
    for r in range(NSA_REP):
        a_e = st_e[:, r * tq:(r + 1) * tq] - slope[r] * (tl - end_et).astype(F32)
        a_o = st_o[:, r * tq:(r + 1) * tq] - slope[r] * (tl - end_ot).astype(F32)
        mt = jnp.maximum(jnp.max(jnp.where(ok_et, a_e, NEG), axis=0, keepdims=True),
                         jnp.max(jnp.where(ok_ot, a_o, NEG), axis=0, keepdims=True))
        mt = jnp.where(mt > 0.5 * NEG, mt, 0.0)
        x_e = jnp.where(ok_et, jnp.exp(a_e - mt), 0.0)
        x_o = jnp.where(ok_ot, jnp.exp(a_o - mt), 0.0)
        dt = jnp.maximum(jnp.sum(x_e, axis=0, keepdims=True) + jnp.sum(x_o, axis=0, keepdims=True), 1e-30)
        score = score + (x_e / dt + x_o / dt)
    cur = tl // SEL_BLOCK
    forced = (jr == 0) | (jr == cur) | (jr == cur - 1)
    score = jnp.where(forced, FORCE_SCORE, jnp.where(jr * SEL_BLOCK <= tl, score, -1.0))
    rank = jnp.zeros((nsel, tq), jnp.int32)
    for j2 in range(nsel):
        other = score[j2:j2 + 1, :]
        ahead = (other > score) | ((other == score) & (jr > j2))
        rank = rank + jnp.where(ahead, 1, 0)
    sel_t = jnp.where(rank < min(TOP_N, nsel), 1.0, 0.0).astype(BF16)
    eye_r = lax.broadcasted_iota(jnp.int32, (tq, tq), 0)
    eye_c = lax.broadcasted_iota(jnp.int32, (tq, tq), 1)
    eye = jnp.where(eye_r == eye_c, 1.0, 0.0).astype(BF16)
    sel = _nt(eye, sel_t).astype(BF16)

    def reset():
        m_ref[...] = jnp.full_like(m_ref, NEG)
        l_ref[...] = jnp.zeros_like(l_ref)
        acc_ref[...] = jnp.zeros_like(acc_ref)

    def scores(k_ref_, j):
        off = pl.multiple_of(j * LANES, LANES)
        k_pos = j * LANES + lane
        dist = t_col - k_pos
        return _nt(qs, k_ref_[pl.ds(off, LANES), :]) - slope_col * dist.astype(F32), dist, off

    per_tile = LANES // SEL_BLOCK
    br = lax.broadcasted_iota(jnp.int32, (nsel, LANES), 0)
    bc = lax.broadcasted_iota(jnp.int32, (nsel, LANES), 1) // SEL_BLOCK
    reset()

    def sel_body(j, c):
        s, dist, off = scores(ksel_ref, j)
        expand = jnp.where(br == j * per_tile + bc, 1.0, 0.0).astype(BF16)
        chosen = _mm(sel, expand)
        chosen = jnp.concatenate([chosen] * NSA_REP, axis=0)
        _flash_tile(s, (chosen > 0.5) & (dist >= 0), vsel_ref[pl.ds(off, LANES), :], m_ref, l_ref, acc_ref)
        return c

    lax.fori_loop(0, (i * tq) // LANES + tq // LANES, sel_body, 0)
    osel_ref[...] = acc_ref[...] / jnp.maximum(l_ref[...], 1e-30)

    reset()

    def win_body(j, c):
        s, dist, off = scores(kwin_ref, j)
        _flash_tile(s, (dist >= 0) & (dist < WINDOW), vwin_ref[pl.ds(off, LANES), :], m_ref, l_ref, acc_ref)
        return c

    first = jnp.maximum(i * tq - (WINDOW - 1), 0) // LANES
    lax.fori_loop(first, (i * tq) // LANES + tq // LANES, win_body, 0)
    o_win = acc_ref[...] / jnp.maximum(l_ref[...], 1e-30)
    o_sel = osel_ref[...]

    for b in range(NSA_REP // 2):
        lo_rows = slice(2 * b * tq, (2 * b + 1) * tq)
        hi_rows = slice((2 * b + 1) * tq, (2 * b + 2) * tq)
        cols = slice(b * LANES, (b + 1) * LANES)
        pick = lambda o: jnp.where(low, o[lo_rows], o[hi_rows])
        merged = (g0_ref[:, cols].astype(F32) * pick(o_cmp) + g1_ref[:, cols].astype(F32) * pick(o_sel)
                  + g2_ref[:, cols].astype(F32) * pick(o_win))
        o_ref[:, cols] = merged.astype(o_ref.dtype)


def _nsa_prompt(slopes, q, cmp_kv, cw, sel_dup, win_dup, gates, n, t, tq):
    nq = t // tq
    gw = NSA_REP * D_HEAD
    row = lambda b, g, i: b * nq + i
    seq = lambda c: pl.BlockSpec((t, LANES), lambda b, g, i, c=c: (b, c * NSA_KV_HEADS + g))
    gate = lambda br: pl.BlockSpec((tq, gw), lambda b, g, i, br=br: (row(b, g, i), br * NSA_KV_HEADS + g))
    nsel = t // SEL_BLOCK
    rows = NSA_REP * tq
    return pl.pallas_call(
        functools.partial(_nsa_prompt_kernel, tq=tq, t_len=t),
        grid=(n, NSA_KV_HEADS, nq),
        in_specs=[pl.BlockSpec(memory_space=pltpu.SMEM),
                  pl.BlockSpec((tq, gw), lambda b, g, i: (row(b, g, i), g)),
                  pl.BlockSpec((t, KV_WIDTH), lambda b, g, i: (b, 0)),
                  pl.BlockSpec((CMP_BLOCK, KV_WIDTH), lambda b, g, i: (0, 0)),
                  seq(0), seq(1), seq(0), seq(1), gate(0), gate(1), gate(2)],
        out_specs=pl.BlockSpec((tq, gw), lambda b, g, i: (row(b, g, i), g)),
        out_shape=jax.ShapeDtypeStruct((n * t, NSA_WIDTH), BF16),
        scratch_shapes=[pltpu.VMEM((4, nsel, LANES), BF16), pltpu.VMEM((rows, 1), F32), pltpu.VMEM((rows, 1), F32),
                        pltpu.VMEM((rows, LANES), F32), pltpu.VMEM((rows, LANES), F32)],
        compiler_params=_params("arbitrary", "arbitrary", "arbitrary"),
        name="nsa_prompt",
    )(slopes, q, cmp_kv, cw, sel_dup, sel_dup, win_dup, win_dup, gates, gates, gates)


def _sb_sample_kernel(pt_ref, q_ref, new_ref, *rest, n_step, ts):
    page_refs, o_ref, acc_ref, carry_ref = rest[:n_step], rest[n_step], rest[n_step + 1], rest[n_step + 2]
    jj = pl.program_id(1)
    rows = ts * SB_HEADS
    rr = lax.broadcasted_iota(jnp.int32, (rows, SB_WIDTH), 0)
    cc = lax.broadcasted_iota(jnp.int32, (rows, SB_WIDTH), 1)
    own_head = (cc // D_HEAD) == lax.rem(rr, SB_HEADS)
    q = q_ref[...]
    qrows = jnp.where(own_head, q, jnp.zeros_like(q))
    later = _later_and_rowsum()

    @pl.when(jj == 0)
    def _first():
        acc_ref[...] = jnp.zeros_like(acc_ref)
        carry_ref[...] = jnp.zeros_like(carry_ref)
        new = new_ref[...]
        s_key = lax.broadcasted_iota(jnp.int32, (rows, PAGE_SIZE), 1)
        t_qry = lax.broadcasted_iota(jnp.int32, (rows, PAGE_SIZE), 0) // SB_HEADS
        _sb_block(_nt(qrows, new[:, :SB_WIDTH]), s_key < t_qry, lambda a: _mm(a, new[:, SB_WIDTH:]),
                  later, acc_ref, carry_ref)

    for u in range(n_step):
        page = page_refs[u]
        v_t = page[SB_WIDTH:, :].astype(BF16)
        _sb_block(_mm(qrows, page[:SB_WIDTH, :].astype(BF16)), None, lambda a, v_t=v_t: _nt(a, v_t),
                  later, acc_ref, carry_ref)

    @pl.when(jj == pl.num_programs(1) - 1)
    def _last():
        own = jnp.where(own_head, acc_ref[...], 0.0)
        o_ref[...] = jnp.sum(own.reshape(ts, SB_HEADS, SB_WIDTH), axis=1).astype(o_ref.dtype)


def _sb_sample(page_table, q_rep, kv_new, cache, n_step):
    ns, rows, _ = q_rep.shape
    ts = rows // SB_HEADS
    n_pages = page_table.shape[1]
    steps = n_pages // n_step

    def page_spec(u):
        def index(b, jj, pt):
            return (pt[b * n_pages + (n_pages - 1 - (jj * n_step + u))], 0, 0)
        return pl.BlockSpec((None, 2 * SB_WIDTH, PAGE_SIZE), index)

    grid_spec = pltpu.PrefetchScalarGridSpec(
        num_scalar_prefetch=1,
        grid=(ns, steps),
        in_specs=[pl.BlockSpec((None, rows, SB_WIDTH), lambda b, jj, pt: (b, 0, 0)),
                  pl.BlockSpec((None, PAGE_SIZE, 2 * SB_WIDTH), lambda b, jj, pt: (b, 0, 0))]
                 + [page_spec(u) for u in range(n_step)],
        out_specs=pl.BlockSpec((None, ts, SB_WIDTH), lambda b, jj, pt: (b, 0, 0)),
        scratch_shapes=[pltpu.VMEM((rows, SB_WIDTH), F32), pltpu.VMEM((rows, LANES), F32)])
    return pl.pallas_call(
        functools.partial(_sb_sample_kernel, n_step=n_step, ts=ts),
        grid_spec=grid_spec,
        out_shape=jax.ShapeDtypeStruct((ns, ts, SB_WIDTH), BF16),
        compiler_params=_params("arbitrary", "arbitrary"),
        name="sb_sample",
    )(page_table.reshape(-1), q_rep, kv_new, *([cache] * n_step))


def _cmp_sample_kernel(pt_ref, slopes_ref, q_ref, cw_ref, *rest, n_step, ts, past):
    page_refs = rest[:n_step]
    o_ref, idx_ref, ck_ref, sc_ref = rest[n_step:]
    jj = pl.program_id(1)
    nps = past // SEL_BLOCK
    rows = q_ref.shape[1]
    width = n_step * PAGE_SIZE

    @pl.when(jj == 0)
    def _zero():
        ck_ref[...] = jnp.zeros_like(ck_ref)

    weighted = jnp.concatenate([r[...] for r in page_refs], axis=1) * cw_ref[...]
    key = lax.broadcasted_iota(jnp.int32, (width, 2 * nps), 0)
    col = lax.broadcasted_iota(jnp.int32, (width, 2 * nps), 1)
    in_page = lax.rem(key, PAGE_SIZE)
    sel_block = (jj * n_step + key // PAGE_SIZE) * (PAGE_SIZE // SEL_BLOCK) + in_page // SEL_BLOCK
    parity = lax.rem(in_page // CMP_BLOCK, 2)
    pool = jnp.where(col == parity * nps + sel_block, 1.0, 0.0).astype(BF16)
    ck_ref[...] += sum(_mm(piece, pool) for piece in _split3_bf16(weighted))

    @pl.when(jj == pl.num_programs(1) - 1)
    def _attend():
        ck = ck_ref[...]
        ke_t, ko_t = ck[:LANES, :nps].astype(BF16), ck[:LANES, nps:].astype(BF16)
        ve_t, vo_t = ck[LANES:, :nps].astype(BF16), ck[LANES:, nps:].astype(BF16)
        jl = lax.broadcasted_iota(jnp.int32, (rows, nps), 1)
        rl = lax.broadcasted_iota(jnp.int32, (rows, nps), 0)
        pos_n = past + lax.rem(rl, ts)
        d_e = pos_n - (jl * SEL_BLOCK + CMP_BLOCK - 1)
        d_o = pos_n - (jl * SEL_BLOCK + SEL_BLOCK - 1)
        p_sum = jnp.zeros((rows, nps), F32)
        for r in range(NSA_REP):
            q = q_ref[r]
            sl = [slopes_ref[g * NSA_REP + r] for g in range(NSA_KV_HEADS)]
            slope = jnp.where(rl // ts == 0, sl[0], sl[1])
            s_e = _mm(q, ke_t) - slope * d_e.astype(F32)
            s_o = _mm(q, ko_t) - slope * d_o.astype(F32)
            mx = jnp.maximum(jnp.max(jnp.where(d_e >= 0, s_e, NEG), axis=1, keepdims=True),
                             jnp.max(jnp.where(d_o >= 0, s_o, NEG), axis=1, keepdims=True))
            mx = jnp.where(mx > 0.5 * NEG, mx, 0.0)
            e_e = jnp.where(d_e >= 0, jnp.exp(s_e - mx), 0.0)
            e_o = jnp.where(d_o >= 0, jnp.exp(s_o - mx), 0.0)
            den = jnp.maximum(jnp.sum(e_e, axis=1, keepdims=True) + jnp.sum(e_o, axis=1, keepdims=True), 1e-30)
            p_e, p_o = e_e / den, e_o / den
            o_ref[r] = _nt(p_e.astype(BF16), ve_t) + _nt(p_o.astype(BF16), vo_t)
            p_sum = p_sum + (p_e + p_o)
        er = lax.broadcasted_iota(jnp.int32, (nps, nps), 0)
        ec = lax.broadcasted_iota(jnp.int32, (nps, nps), 1)
        eye = jnp.where(er == ec, 1.0, 0.0).astype(BF16)
        score = sum(_nt(eye, piece) for piece in _split3_bf16(p_sum))
        jr = lax.broadcasted_iota(jnp.int32, (nps, rows), 0)
        cl = lax.broadcasted_iota(jnp.int32, (nps, rows), 1)
        pos_t = past + lax.rem(cl, ts)
        cur = pos_t // SEL_BLOCK
        forced = (jr == 0) | (jr == cur) | (jr == cur - 1)
        score = jnp.where(forced, FORCE_SCORE, jnp.where(jr * SEL_BLOCK <= pos_t, score, -1.0))
        pos_1 = past + lax.rem(lax.broadcasted_iota(jnp.int32, (1, rows), 1), ts)
        cur_1 = pos_1 // SEL_BLOCK
        forced_1 = (cur_1 == nps) | (cur_1 - 1 == nps)
        tail = jnp.where(forced_1, FORCE_SCORE, jnp.where(nps * SEL_BLOCK <= pos_1, 0.0, -1.0))
        sc_ref[...] = score

        def count(j2, rank):
            other = sc_ref[pl.ds(j2, 1), :]
            ahead = (other > score) | ((other == score) & (jr > j2))
            return rank + jnp.where(ahead, 1.0, 0.0)

        rank = lax.fori_loop(0, nps, count, jnp.where(tail > score, 1.0, 0.0))
        rank_tail = jnp.sum(jnp.where(score >= tail, 1.0, 0.0), axis=0, keepdims=True)
        jf = jr.astype(F32)
        for k in range(TOP_N):
            hit = jnp.sum(jnp.where(rank == float(k), jf, 0.0), axis=0, keepdims=True)
            hit = hit + jnp.where(rank_tail == float(k), float(nps), 0.0)
            idx_ref[k:k + 1, :] = hit.astype(jnp.int32)


def _cmp_sample(page_table, slopes, q, cw_t, cache, n_step, ts):
    ns, _, rows, _ = q.shape
    n_pages = page_table.shape[1]
    steps = n_pages // n_step
    past = n_pages * PAGE_SIZE
    nps = past // SEL_BLOCK

    def page_spec(u):
        return pl.BlockSpec((None, KV_WIDTH, PAGE_SIZE),
                            lambda b, jj, pt: (pt[b * n_pages + jj * n_step + u], 0, 0))

    grid_spec = pltpu.PrefetchScalarGridSpec(
        num_scalar_prefetch=1,
        grid=(ns, steps),
        in_specs=[pl.BlockSpec(memory_space=pltpu.SMEM),
                  pl.BlockSpec((None, NSA_REP, rows, LANES), lambda b, jj, pt: (b, 0, 0, 0)),
                  pl.BlockSpec(cw_t.shape, lambda b, jj, pt: (0, 0))]
                 + [page_spec(u) for u in range(n_step)],
        out_specs=[pl.BlockSpec((None, NSA_REP, rows, LANES), lambda b, jj, pt: (b, 0, 0, 0)),
                   pl.BlockSpec((None, TOP_N, rows), lambda b, jj, pt: (b, 0, 0))],
        scratch_shapes=[pltpu.VMEM((KV_WIDTH, 2 * nps), F32), pltpu.VMEM((nps, rows), F32)])
    return pl.pallas_call(
        functools.partial(_cmp_sample_kernel, n_step=n_step, ts=ts, past=past),
        grid_spec=grid_spec,
        out_shape=[jax.ShapeDtypeStruct((ns, NSA_REP, rows, LANES), F32),
                   jax.ShapeDtypeStruct((ns, TOP_N, rows), jnp.int32)],
        compiler_params=_params("arbitrary", "arbitrary"),
        name="cmp_sample",
    )(page_table.reshape(-1), slopes, q, cw_t, *([cache] * n_step))


def _sel_sample_kernel(pt_ref, idx_ref, slopes_ref, q_ref, new_ref, *rest, ts, past, cols):
    page_refs, o_ref = rest[:TOP_N], rest[TOP_N]
    b, g, t = pl.program_id(0), pl.program_id(1), pl.program_id(2)
    nps = past // SEL_BLOCK
    per_page = PAGE_SIZE // SEL_BLOCK
    rows = q_ref.shape[0]
    width = TOP_N * PAGE_SIZE
    pos = past + t
    q = q_ref[...]
    rr = lax.broadcasted_iota(jnp.int32, (rows, 1), 0)
    slope = jnp.zeros((rows, 1), F32)
    for r in range(NSA_REP):
        slope = jnp.where(rr == r, slopes_ref[g * NSA_REP + r], slope)
    its = [idx_ref[(b * TOP_N + k) * cols + g * ts + t] for k in range(TOP_N)]
    ll = lax.broadcasted_iota(jnp.int32, (1, width), 1)
    block_of = jnp.zeros((1, width), jnp.int32)
    for k in range(TOP_N):
        block_of = jnp.where(ll // PAGE_SIZE == k, its[k], block_of)
    in_page = lax.rem(ll, PAGE_SIZE)
    k_pos = block_of * SEL_BLOCK + lax.rem(in_page, SEL_BLOCK)
    ok = (block_of < nps) & (in_page // SEL_BLOCK == lax.rem(block_of, per_page)) & (k_pos <= pos)
    k_t = jnp.concatenate([r[:LANES, :] for r in page_refs], axis=1).astype(BF16)
    v_t = jnp.concatenate([r[LANES:, :] for r in page_refs], axis=1).astype(BF16)
    s = _mm(q, k_t) - slope * (pos - k_pos).astype(F32)
    n_sel_new = jnp.where(its[0] >= nps, 1, 0)
    for k in range(1, TOP_N):
        n_sel_new = n_sel_new + jnp.where(its[k] >= nps, 1, 0)
    limit = jnp.where(n_sel_new > 0, pos, past - 1)
    new = new_ref[...]
    n_new = new.shape[0]
    kn_pos = past + lax.broadcasted_iota(jnp.int32, (1, n_new), 1)
    ok_n = kn_pos <= limit
    s_n = _nt(q, new[:, :LANES].astype(BF16)) - slope * (pos - kn_pos).astype(F32)
    mx = jnp.maximum(jnp.max(jnp.where(ok, s, NEG), axis=1, keepdims=True),
                     jnp.max(jnp.where(ok_n, s_n, NEG), axis=1, keepdims=True))
    mx = jnp.where(mx > 0.5 * NEG, mx, 0.0)
    e = jnp.where(ok, jnp.exp(s - mx), 0.0)
    e_n = jnp.where(ok_n, jnp.exp(s_n - mx), 0.0)
    den = jnp.maximum(jnp.sum(e, axis=1, keepdims=True) + jnp.sum(e_n, axis=1, keepdims=True), 1e-30)
    o_ref[...] = _nt((e / den).astype(BF16), v_t) + _mm((e_n / den).astype(BF16), new[:, LANES:].astype(BF16))


def _sel_sample(page_table, idx, slopes, q, kv_new, cache, ts):
    ns = q.shape[0]
    n_pages = page_table.shape[1]
    past = n_pages * PAGE_SIZE
    nps = past // SEL_BLOCK
    per_page = PAGE_SIZE // SEL_BLOCK
    cols = idx.shape[2]
    rows = q.shape[3]

    def page_spec(k):
        def index(b, g, t, pt, ix):
            it = jnp.minimum(ix[(b * TOP_N + k) * cols + g * ts + t], nps - 1)
            return (pt[b * n_pages + it // per_page], 0, 0)
        return pl.BlockSpec((None, KV_WIDTH, PAGE_SIZE), index)

    grid_spec = pltpu.PrefetchScalarGridSpec(
        num_scalar_prefetch=2,
        grid=(ns, NSA_KV_HEADS, ts),
        in_specs=[pl.BlockSpec(memory_space=pltpu.SMEM),
                  pl.BlockSpec((None, None, None, rows, LANES), lambda b, g, t, pt, ix: (b, g, t, 0, 0)),
                  pl.BlockSpec((None,) + kv_new.shape[1:], lambda b, g, t, pt, ix: (b, 0, 0))]
                 + [page_spec(k) for k in range(TOP_N)],
        out_specs=pl.BlockSpec((None, None, None, rows, LANES), lambda b, g, t, pt, ix: (b, g, t, 0, 0)))
    return pl.pallas_call(
        functools.partial(_sel_sample_kernel, ts=ts, past=past, cols=cols),
        grid_spec=grid_spec,
        out_shape=jax.ShapeDtypeStruct((ns, NSA_KV_HEADS, ts, rows, LANES), F32),
        compiler_params=_params("arbitrary", "arbitrary", "arbitrary"),
        name="sel_sample",
    )(page_table.reshape(-1), idx.reshape(-1), slopes, q, kv_new, *([cache] * TOP_N))


def _win_sample_kernel(slopes_ref, q_ref, buf_ref, new_ref, o_ref, *, ts, past):
    rows = NSA_HEADS * ts
    w = buf_ref.shape[1]
    rr = lax.broadcasted_iota(jnp.int32, (rows, 1), 0)
    slope = jnp.zeros((rows, 1), F32)
    for h in range(NSA_HEADS):
        slope = jnp.where(rr // ts == h, slopes_ref[h], slope)
    pos = past + lax.rem(rr, ts)
    q = q_ref[...]
    new = new_ref[...]
    n_new = new.shape[0]
    kb_pos = past - w + lax.broadcasted_iota(jnp.int32, (1, w), 1)
    kn_pos = past + lax.broadcasted_iota(jnp.int32, (1, n_new), 1)
    d_b = pos - kb_pos
    d_n = pos - kn_pos
    ok_b = (d_b >= 0) & (d_b < WINDOW) & (kb_pos >= 0)
    ok_n = (d_n >= 0) & (d_n < WINDOW)
    s_b = _mm(q, buf_ref[:LANES, :].astype(BF16)) - slope * d_b.astype(F32)
    s_n = _nt(q, new[:, :LANES].astype(BF16)) - slope * d_n.astype(F32)
    mx = jnp.maximum(jnp.max(jnp.where(ok_b, s_b, NEG), axis=1, keepdims=True),
                     jnp.max(jnp.where(ok_n, s_n, NEG), axis=1, keepdims=True))
    mx = jnp.where(mx > 0.5 * NEG, mx, 0.0)
    e_b = jnp.where(ok_b, jnp.exp(s_b - mx), 0.0)
    e_n = jnp.where(ok_n, jnp.exp(s_n - mx), 0.0)
    den = jnp.maximum(jnp.sum(e_b, axis=1, keepdims=True) + jnp.sum(e_n, axis=1, keepdims=True), 1e-30)
    o_ref[...] = (_nt((e_b / den).astype(BF16), buf_ref[LANES:, :].astype(BF16))
                  + _mm((e_n / den).astype(BF16), new[:, LANES:].astype(BF16)))


def _win_sample(slopes, q, win_buf_t, kv_new, ts, past):
    ns, rows, _ = q.shape
    return pl.pallas_call(
        functools.partial(_win_sample_kernel, ts=ts, past=past),
        grid=(ns,),
        in_specs=[pl.BlockSpec(memory_space=pltpu.SMEM),
                  pl.BlockSpec((None, rows, LANES), lambda b: (b, 0, 0)),
                  pl.BlockSpec((None,) + win_buf_t.shape[1:], lambda b: (b, 0, 0)),
                  pl.BlockSpec((None,) + kv_new.shape[1:], lambda b: (b, 0, 0))],
        out_specs=pl.BlockSpec((None, rows, LANES), lambda b: (b, 0, 0)),
        out_shape=jax.ShapeDtypeStruct((ns, rows, LANES), F32),
        compiler_params=_params("arbitrary"),
        name="win_sample",
    )(slopes, q, win_buf_t, kv_new)


def _merge_sample_kernel(cmp_ref, sel_ref, win_ref, gate_ref, o_ref):
    lane = lax.broadcasted_iota(jnp.int32, (1, LANES), 1)
    low = lane < D_HEAD
    for g in range(NSA_KV_HEADS):
        for b in range(NSA_REP // 2):
            col = g * NSA_REP * D_HEAD + b * LANES
            total = None
            for br, ref in enumerate((cmp_ref, sel_ref, win_ref)):
                first, second = ref[g, 2 * b], ref[g, 2 * b + 1]
                if g == 0:
                    val = jnp.where(low, first, pltpu.roll(second, D_HEAD, 1))
                else:
                    val = jnp.where(low, pltpu.roll(first, D_HEAD, 1), second)
                gate = gate_ref[:, br * NSA_WIDTH + col: br * NSA_WIDTH + col + LANES].astype(F32)
                total = gate * val if total is None else total + gate * val
            o_ref[:, col:col + LANES] = total.astype(o_ref.dtype)


def _merge_sample(o_cmp, o_sel, o_win, gates):
    rows = gates.shape[0]
    full = lambda a: pl.BlockSpec(a.shape, lambda i: (0,) * a.ndim)
    return pl.pallas_call(
        _merge_sample_kernel,
        grid=(1,),
        in_specs=[full(o_cmp), full(o_sel), full(o_win), full(gates)],
        out_specs=pl.BlockSpec((rows, NSA_WIDTH), lambda i: (0, 0)),
        out_shape=jax.ShapeDtypeStruct((rows, NSA_WIDTH), BF16),
        compiler_params=_params("arbitrary"),
        name="merge_sample",
    )(o_cmp, o_sel, o_win, gates)


def _dup_kv_cols(w):
    d = w.shape[0]
    return jnp.repeat(w.reshape(d, 2 * NSA_KV_HEADS, 1, D_HEAD), 2, axis=2).reshape(d, 2 * KV_WIDTH)


def _gate_cols(w):
    d = w.shape[0]
    per = w.reshape(d, NSA_HEADS, N_NSA_BRANCHES).transpose(0, 2, 1)
    return jnp.repeat(per[..., None], D_HEAD, axis=3).reshape(d, N_NSA_BRANCHES * NSA_WIDTH)


def _expand_q_cols(w):
    d = w.shape[0]
    place = jax.nn.one_hot(jnp.arange(NSA_HEADS) // NSA_REP, NSA_KV_HEADS, dtype=w.dtype)
    return (w.reshape(d, NSA_HEADS, 1, D_HEAD) * place[None, :, :, None]).reshape(d, NSA_HEADS * LANES)


def _pad_rows(a, rows):
    return jnp.pad(a, [(0, 0)] * (a.ndim - 2) + [(0, rows - a.shape[-2]), (0, 0)])


def _keys_on_lanes(a):
    b, s = a.shape[:2]
    return jnp.transpose(a, (0, 2, 3, 4, 1)).reshape(b, -1, s)


def _groups(spec):
    groups, c = [], 0
    for width, act, dts in spec:
        groups.append((c, width, act, dts))
        c += width
    return tuple(groups)


def _prompt_layer(x, mods, lw, slopes, n, t, final):
    (w_in, g_mix, g_mlp, g_final, cmp_w, w_br_sb, w_br_nsa, w_out, w_up, w_down) = lw
    sh1, sc1, gt1, sh2, sc2, gt2 = mods
    tm = min(256, t)
    w_cat = jnp.concatenate([
        w_in[:, _C_SBQ:_C_SBKV] * ATTN_SCALE, w_in[:, _C_SBKV:_C_NSQ], w_in[:, _C_NSQ:_C_CMP] * ATTN_SCALE,
        w_in[:, _C_CMP:_C_GATE], _dup_kv_cols(w_in[:, _C_SEL:_C_WIN]), _dup_kv_cols(w_in[:, _C_WIN:_C_GATE]),
        _gate_cols(w_in[:, _C_GATE:_C_GSB])], axis=1).astype(BF16)
    groups = _groups(((SB_WIDTH, None, (BF16,)), (2 * SB_WIDTH, None, (F32, BF16)), (NSA_WIDTH, None, (BF16,)),
                      (KV_WIDTH, None, (F32,)), (KV_WIDTH, None, (F32,)), (KV_WIDTH, None, (F32,)),
                      (2 * KV_WIDTH, None, (BF16,)), (2 * KV_WIDTH, None, (BF16,)),
                      (N_NSA_BRANCHES * NSA_WIDTH, "sigmoid", (BF16,))))
    (sbq, sbkv, sbkv_b, nsq, cmp_kv, sel_kv, win_kv, sel_dup, win_dup, gates) = _project(
        x, g_mix, sc1, sh1, w_cat, groups, tm, t)
    o_sb = _sb_prompt(sbq, sbkv_b, n, t, min(256, t))
    cw = jnp.repeat(cmp_w.reshape(CMP_BLOCK, 2 * NSA_KV_HEADS), D_HEAD, axis=1)
    o_nsa = _nsa_prompt(slopes, nsq, cmp_kv, cw, sel_dup, win_dup, gates, n, t, LANES)
    x1 = _mix(x, g_mix, sc1, sh1, gt1, o_sb, o_nsa, w_in[:, _C_GSB:].astype(BF16), w_br_sb.astype(BF16),
              w_br_nsa.astype(BF16), w_out.astype(BF16), tm, t)
    x2 = _mlp(x1, g_mlp, sc2, sh2, gt2, g_final, w_up.astype(BF16), w_down.astype(BF16), tm, t, final)
    return x2, (sbkv, cmp_kv, sel_kv, win_kv)


def _sample_mixers(sbq, sbkv_b, nsqx, sel_kv, win_kv, gates, cmp_w, slopes, caches, page_table, ns, ts):
    cache_sb, cache_cmp, cache_sel, win_buf = caches
    m = ns * ts
    n_pages = page_table.shape[1]
    past = n_pages * PAGE_SIZE
    n_step = math.gcd(n_pages, 8)
    g_, r_ = NSA_KV_HEADS, NSA_REP
    q_rep = jnp.broadcast_to(sbq.reshape(ns, ts, 1, SB_WIDTH), (ns, ts, SB_HEADS, SB_WIDTH))
    q_rep = q_rep.reshape(ns, ts * SB_HEADS, SB_WIDTH)
    kv_new = _pad_rows(sbkv_b.reshape(ns, ts, 2 * SB_WIDTH), PAGE_SIZE)
    o_sb = _sb_sample(page_table, q_rep, kv_new, _keys_on_lanes(cache_sb), n_step).reshape(m, SB_WIDTH)

    qx = nsqx.reshape(ns, ts, g_, r_, LANES)
    q_cmp = _pad_rows(qx.transpose(0, 3, 2, 1, 4).reshape(ns, r_, g_ * ts, LANES), 16)
    q_sel = _pad_rows(qx.transpose(0, 2, 1, 3, 4), 16)
    q_win = qx.transpose(0, 2, 3, 1, 4).reshape(ns, NSA_HEADS * ts, LANES)
    cw_t = jnp.tile(jnp.repeat(cmp_w.reshape(CMP_BLOCK, 2 * g_).T, D_HEAD, axis=0), (1, n_step * PAGE_SIZE // CMP_BLOCK))
    o_cmp, idx = _cmp_sample(page_table, slopes, q_cmp, cw_t, _keys_on_lanes(cache_cmp), n_step, ts)
    kv_rows = lambda a: _pad_rows(a.reshape(ns, ts, KV_WIDTH), 16)
    o_sel = _sel_sample(page_table, idx[:, :, :g_ * ts], slopes, q_sel, kv_rows(sel_kv), _keys_on_lanes(cache_sel), ts)
    o_win = _win_sample(slopes, q_win, _keys_on_lanes(win_buf), kv_rows(win_kv), ts, past)
    oc = o_cmp[:, :, :g_ * ts].reshape(ns, r_, g_, ts, LANES).transpose(2, 1, 0, 3, 4).reshape(g_, r_, m, LANES)
    os_ = o_sel[:, :, :, :r_].transpose(1, 3, 0, 2, 4).reshape(g_, r_, m, LANES)
    ow = o_win.reshape(ns, g_, r_, ts, LANES).transpose(1, 2, 0, 3, 4).reshape(g_, r_, m, LANES)
    return o_sb, _merge_sample(oc, os_, ow, gates)


def _sample_layer(x, mods, lw, slopes, caches, page_table, ns, ts, final):
    (w_in, g_mix, g_mlp, g_final, cmp_w, w_br_sb, w_br_nsa, w_out, w_up, w_down) = lw
    sh1, sc1, gt1, sh2, sc2, gt2 = mods
    tm = sc1.shape[1]
    w_cat = jnp.concatenate([
        w_in[:, _C_SBQ:_C_SBKV] * ATTN_SCALE, w_in[:, _C_SBKV:_C_NSQ],
        _expand_q_cols(w_in[:, _C_NSQ:_C_CMP] * ATTN_SCALE), w_in[:, _C_CMP:_C_GATE],
        _gate_cols(w_in[:, _C_GATE:_C_GSB])], axis=1).astype(BF16)
    groups = _groups(((SB_WIDTH, None, (BF16,)), (2 * SB_WIDTH, None, (F32, BF16)), (NSA_HEADS * LANES, None, (BF16,)),
                      (KV_WIDTH, None, (F32,)), (KV_WIDTH, None, (F32,)), (KV_WIDTH, None, (F32,)),
                      (N_NSA_BRANCHES * NSA_WIDTH, "sigmoid", (BF16,))))
    sbq, sbkv, sbkv_b, nsqx, cmp_kv, sel_kv, win_kv, gates = _project(x, g_mix, sc1, sh1, w_cat, groups, tm, ts)
    o_sb, o_nsa = _sample_mixers(sbq, sbkv_b, nsqx, sel_kv, win_kv, gates, cmp_w, slopes, caches, page_table, ns, ts)
    x1 = _mix(x, g_mix, sc1, sh1, gt1, o_sb, o_nsa, w_in[:, _C_GSB:].astype(BF16), w_br_sb.astype(BF16),
              w_br_nsa.astype(BF16), w_out.astype(BF16), tm, ts)
    x2 = _mlp(x1, g_mlp, sc2, sh2, gt2, g_final, w_up.astype(BF16), w_down.astype(BF16), tm, ts, final)
    return x2, (sbkv, cmp_kv, sel_kv, win_kv)


def kernel(x_prompt, x_sample, cache_sb_kv, cache_cmp_kv, cache_sel_kv, state_win_kv, page_table, c_prompt, c_sample,
           w_ada, b_ada, g_mix, g_mlp, g_final, w_in, cmp_w, w_br_sb, w_br_nsa, w_out, w_up, w_down):
    depth = w_in.shape[0]
    n, t, d = x_prompt.shape
    ns, ts, _ = x_sample.shape
    slopes = jnp.exp2(-8.0 * (jnp.arange(NSA_HEADS, dtype=F32) + 1.0) / NSA_HEADS)
    xp = x_prompt.reshape(n * t, d)
    xs = x_sample.reshape(ns * ts, d)
    ms = ns * ts
    tms = min(LANES, ms)
    c_all = jnp.concatenate([c_prompt, c_sample], axis=0)
    st_p = ([], [], [], [])
    st_s = ([], [], [], [])
    for l in range(depth):
        final = l == depth - 1
        lw = (w_in[l], g_mix[l].reshape(1, d), g_mlp[l].reshape(1, d), g_final.reshape(1, d), cmp_w[l],
              w_br_sb[l], w_br_nsa[l], w_out[l], w_up[l], w_down[l])
        mod = _adaln(c_all, w_ada[l], b_ada[l])
        mods_p = [mod[:n, k * d:(k + 1) * d].reshape(n, 1, d) for k in range(6)]
        mods_s = [jnp.repeat(mod[n:, k * d:(k + 1) * d], ts, axis=0).reshape(ms // tms, tms, d) for k in range(6)]
        xp, sp = _prompt_layer(xp, mods_p, lw, slopes, n, t, final)
        caches = (cache_sb_kv[l], cache_cmp_kv[l], cache_sel_kv[l], state_win_kv[l])
        xs, ss = _sample_layer(xs, mods_s, lw, slopes, caches, page_table, ns, ts, final)
        kv_p = (n, t, 2, NSA_KV_HEADS, D_HEAD)
        kv_s = (ns, ts, 2, NSA_KV_HEADS, D_HEAD)
        win_new = jnp.concatenate([state_win_kv[l], ss[3].reshape(kv_s)], axis=1)
        sp = (sp[0].reshape(n, t, 2, SB_HEADS, D_HEAD), sp[1].reshape(kv_p), sp[2].reshape(kv_p),
              sp[3].reshape(kv_p)[:, t - min(WINDOW, t):])
        ss = (ss[0].reshape(ns, ts, 2, SB_HEADS, D_HEAD), ss[1].reshape(kv_s), ss[2].reshape(kv_s),
              win_new[:, win_new.shape[1] - min(WINDOW, win_new.shape[1]):])
        for acc, s in zip(st_p, sp):
            acc.append(s)
        for acc, s in zip(st_s, ss):
            acc.append(s)
    return (xp.reshape(n, t, d), xs.reshape(ns, ts, d), jnp.stack(st_p[0]), jnp.stack(st_s[0]),
            jnp.stack(st_p[1]), jnp.stack(st_s[1]), jnp.stack(st_p[2]), jnp.stack(st_s[2]),
            jnp.stack(st_p[3]), jnp.stack(st_s[3]))
```

```python
import functools
import math

import jax
import jax.numpy as jnp
from jax import lax
from jax.experimental import pallas as pl
from jax.experimental.pallas import tpu as pltpu

F32 = jnp.float32
BF16 = jnp.bfloat16

D_HEAD = 64
SB_HEADS = 8
NSA_HEADS = 8
NSA_KV_HEADS = 2
NSA_REP = NSA_HEADS // NSA_KV_HEADS
SB_WIDTH = SB_HEADS * D_HEAD
NSA_WIDTH = NSA_HEADS * D_HEAD
KV_WIDTH = 2 * NSA_KV_HEADS * D_HEAD
N_NSA_BRANCHES = 3
CMP_BLOCK = 32
SEL_BLOCK = 64
TOP_N = 16
WINDOW = 512
PAGE_SIZE = 128
RMS_EPS = 1e-6
FORCE_SCORE = 1e4
ATTN_SCALE = 0.125
LANES = 128
NEG = -1e30
SB_GROUP = 4
KEY_GROUP = 256
VMEM_LIMIT = 56 * 1024 * 1024

_C_SBQ = 0
_C_SBKV = _C_SBQ + SB_WIDTH
_C_NSQ = _C_SBKV + 2 * SB_WIDTH
_C_CMP = _C_NSQ + NSA_WIDTH
_C_SEL = _C_CMP + KV_WIDTH
_C_WIN = _C_SEL + KV_WIDTH
_C_GATE = _C_WIN + KV_WIDTH
_C_GSB = _C_GATE + N_NSA_BRANCHES * NSA_HEADS


def _params(*sem):
    return pltpu.CompilerParams(dimension_semantics=sem, vmem_limit_bytes=VMEM_LIMIT)


def _nt(a, b):
    return lax.dot_general(a, b, (((1,), (1,)), ((), ())), preferred_element_type=F32)


def _mm(a, b):
    return jnp.dot(a, b, preferred_element_type=F32)


def _modulated_norm(x, g, sc, sh):
    y = x * lax.rsqrt(jnp.mean(x * x, axis=-1, keepdims=True) + RMS_EPS)
    return (y * g) * (1.0 + sc) + sh


def _adaln_kernel(c_ref, w_ref, b_ref, o_ref):
    c = c_ref[...]
    s = c * jax.nn.sigmoid(c)
    o_ref[...] = _mm(s.astype(BF16), w_ref[...].astype(BF16)) + b_ref[...]


def _adaln(c, w_ada, b_ada):
    n, d = c.shape
    n_out = w_ada.shape[1]
    tn = 512
    return pl.pallas_call(
        _adaln_kernel,
        grid=(n_out // tn,),
        in_specs=[pl.BlockSpec((n, d), lambda j: (0, 0)),
                  pl.BlockSpec((d, tn), lambda j: (0, j)),
                  pl.BlockSpec((1, tn), lambda j: (0, j))],
        out_specs=pl.BlockSpec((n, tn), lambda j: (0, j)),
        out_shape=jax.ShapeDtypeStruct((n, n_out), F32),
        compiler_params=_params("arbitrary"),
        name="adaln",
    )(c, w_ada, b_ada.reshape(1, n_out))


def _proj_kernel(x_ref, g_ref, sc_ref, sh_ref, w_ref, *out_refs, groups):
    hb = _modulated_norm(x_ref[...], g_ref[...], sc_ref[...], sh_ref[...]).astype(BF16)
    k = 0
    for start, width, act, dtypes in groups:
        z = _mm(hb, w_ref[:, start:start + width])
        if act == "sigmoid":
            z = jax.nn.sigmoid(z)
        for dt in dtypes:
            out_refs[k][...] = z.astype(dt)
            k += 1


def _mod_spec(mod, tm, rows_per_seq):
    d = mod.shape[-1]
    if mod.shape[1] == 1:
        tiles = rows_per_seq // tm
        return pl.BlockSpec((None, 1, d), lambda i: (i // tiles, 0, 0))
    return pl.BlockSpec((None, tm, d), lambda i: (i, 0, 0))


def _project(x, g, sc, sh, w_cat, groups, tm, rows_per_seq):
    m, d = x.shape
    out_shape, out_specs = [], []
    for _, width, _, dtypes in groups:
        for dt in dtypes:
            out_shape.append(jax.ShapeDtypeStruct((m, width), dt))
            out_specs.append(pl.BlockSpec((tm, width), lambda i: (i, 0)))
    return pl.pallas_call(
        functools.partial(_proj_kernel, groups=groups),
        grid=(m // tm,),
        in_specs=[pl.BlockSpec((tm, d), lambda i: (i, 0)),
                  pl.BlockSpec((1, d), lambda i: (0, 0)),
                  _mod_spec(sc, tm, rows_per_seq), _mod_spec(sh, tm, rows_per_seq),
                  pl.BlockSpec(w_cat.shape, lambda i: (0, 0))],
        out_specs=out_specs,
        out_shape=out_shape,
        compiler_params=_params("arbitrary"),
        name="in_proj",
    )(x, g, sc, sh, w_cat)


def _mix_kernel(x_ref, g_ref, sc_ref, sh_ref, gt_ref, osb_ref, onsa_ref, wg_ref, wbs_ref, wbn_ref, wo_ref, o_ref):
    x = x_ref[...]
    d = x.shape[1]
    hb = _modulated_norm(x, g_ref[...], sc_ref[...], sh_ref[...]).astype(BF16)
    gate_sb = jax.nn.sigmoid(_mm(hb, wg_ref[:, :d]))
    gate_nsa = jax.nn.sigmoid(_mm(hb, wg_ref[:, d:]))
    mixed = gate_sb * _mm(osb_ref[...], wbs_ref[...]) + gate_nsa * _mm(onsa_ref[...], wbn_ref[...])
    o_ref[...] = x + gt_ref[...] * _mm(mixed.astype(BF16), wo_ref[...])


def _mix(x, g, sc, sh, gt, o_sb, o_nsa, w_gate, w_br_sb, w_br_nsa, w_out, tm, rows_per_seq):
    m, d = x.shape
    full = lambda a: pl.BlockSpec(a.shape, lambda i: (0, 0))
    return pl.pallas_call(
        _mix_kernel,
        grid=(m // tm,),
        in_specs=[pl.BlockSpec((tm, d), lambda i: (i, 0)), pl.BlockSpec((1, d), lambda i: (0, 0)),
                  _mod_spec(sc, tm, rows_per_seq), _mod_spec(sh, tm, rows_per_seq), _mod_spec(gt, tm, rows_per_seq),
                  pl.BlockSpec((tm, o_sb.shape[1]), lambda i: (i, 0)),
                  pl.BlockSpec((tm, o_nsa.shape[1]), lambda i: (i, 0)),
                  full(w_gate), full(w_br_sb), full(w_br_nsa), full(w_out)],
        out_specs=pl.BlockSpec((tm, d), lambda i: (i, 0)),
        out_shape=jax.ShapeDtypeStruct((m, d), F32),
        compiler_params=_params("arbitrary"),
        name="mix_out",
    )(x, g, sc, sh, gt, o_sb, o_nsa, w_gate, w_br_sb, w_br_nsa, w_out)


def _mlp_kernel(x_ref, g_ref, sc_ref, sh_ref, gt_ref, gf_ref, wu_ref, wd_ref, o_ref, *, final):
    x = x_ref[...]
    hb = _modulated_norm(x, g_ref[...], sc_ref[...], sh_ref[...]).astype(BF16)
    u = jnp.square(jnp.maximum(_mm(hb, wu_ref[...]), 0.0))
    x2 = x + gt_ref[...] * _mm(u.astype(BF16), wd_ref[...])
    if final:
        x2 = x2 * lax.rsqrt(jnp.mean(x2 * x2, axis=-1, keepdims=True) + RMS_EPS) * gf_ref[...]
    o_ref[...] = x2


def _mlp(x, g, sc, sh, gt, g_final, w_up, w_down, tm, rows_per_seq, final):
    m, d = x.shape
    full = lambda a: pl.BlockSpec(a.shape, lambda i: (0, 0))
    return pl.pallas_call(
        functools.partial(_mlp_kernel, final=final),
        grid=(m // tm,),
        in_specs=[pl.BlockSpec((tm, d), lambda i: (i, 0)), pl.BlockSpec((1, d), lambda i: (0, 0)),
                  _mod_spec(sc, tm, rows_per_seq), _mod_spec(sh, tm, rows_per_seq), _mod_spec(gt, tm, rows_per_seq),
                  pl.BlockSpec((1, d), lambda i: (0, 0)), full(w_up), full(w_down)],
        out_specs=pl.BlockSpec((tm, d), lambda i: (i, 0)),
        out_shape=jax.ShapeDtypeStruct((m, d), F32),
        compiler_params=_params("arbitrary"),
        name="mlp_final",
    )(x, g, sc, sh, gt, g_final, w_up, w_down)


def _log_sigmoids(z):
    sp = jnp.log(1.0 + jnp.exp(-jnp.abs(z)))
    return jnp.minimum(z, 0.0) - sp, -jnp.maximum(z, 0.0) - sp


def _split_bf16(x):
    hi = x.astype(BF16)
    return hi, (x - hi.astype(F32)).astype(BF16)


def _split3_bf16(x):
    hi = x.astype(BF16)
    r1 = x - hi.astype(F32)
    mid = r1.astype(BF16)
    return hi, mid, (r1 - mid.astype(F32)).astype(BF16)


def _later_and_rowsum():
    r = lax.broadcasted_iota(jnp.int32, (2 * LANES, 2 * LANES), 0)
    c = lax.broadcasted_iota(jnp.int32, (2 * LANES, 2 * LANES), 1)
    return jnp.where((c >= LANES) | ((r & (LANES - 1)) > c), 1.0, 0.0).astype(BF16)


def _sb_group(z, msk, pv, later, acc_ref, carry_ref, latest_first):
    rows = z.shape[0]
    nb = z.shape[1] // LANES
    lsz, l1m = _log_sigmoids(z)
    if msk is not None:
        l1m = jnp.where(msk, l1m, 0.0)
    hi, lo = _split_bf16(l1m)
    stack = lambda x: jnp.concatenate([x[:, u * LANES:(u + 1) * LANES] for u in range(nb)], axis=0)
    cs = _mm(jnp.concatenate([stack(hi), stack(lo)], axis=1), later)
    carry = carry_ref[...]
    parts = [None] * nb
    for u in (range(nb) if latest_first else reversed(range(nb))):
        c_u = cs[u * rows:(u + 1) * rows]
        parts[u] = lsz[:, u * LANES:(u + 1) * LANES] + c_u[:, :LANES] + carry
        carry = carry + c_u[:, LANES:]
    carry_ref[...] = carry
    a = jnp.exp(jnp.concatenate(parts, axis=1))
    if msk is not None:
        a = jnp.where(msk, a, 0.0)
    acc_ref[...] += pv(a.astype(BF16))


def _sb_prompt_kernel(q_ref, k_ref, v_ref, o_ref, acc_ref, carry_ref, *, tq):
    i = pl.program_id(2)
    nb = tq // LANES
    lane = lax.broadcasted_iota(jnp.int32, (1, LANES), 1)
    q = q_ref[...]
    zero = jnp.zeros_like(q)
    qs = jnp.concatenate([jnp.where(lane < D_HEAD, q, zero), jnp.where(lane >= D_HEAD, q, zero)], axis=0)
    later = _later_and_rowsum()
    q_pos = i * tq + lax.rem(lax.broadcasted_iota(jnp.int32, (2 * tq, 1), 0), tq)
    acc_ref[...] = jnp.zeros_like(acc_ref)
    carry_ref[...] = jnp.zeros_like(carry_ref)

    def group(j0, n_tiles, masked):
        off = pl.multiple_of(j0 * LANES, LANES)
        kb = k_ref[pl.ds(off, n_tiles * LANES), :]
        vb = v_ref[pl.ds(off, n_tiles * LANES), :]
        msk = None
        if masked:
            msk = (off + lax.broadcasted_iota(jnp.int32, (1, n_tiles * LANES), 1)) < q_pos
        _sb_group(_nt(qs, kb), msk, lambda a: _mm(a, vb), later, acc_ref, carry_ref, latest_first=False)

    group(i * nb, nb, True)
    below = i * nb
    rem = lax.rem(below, SB_GROUP)
    for r in range(1, SB_GROUP):
        @pl.when(rem == r)
        def _(r=r):
            group(below - r, r, False)

    def body(jj, c):
        group(below - rem - SB_GROUP * (jj + 1), SB_GROUP, False)
        return c

    lax.fori_loop(0, below // SB_GROUP, body, 0)
    acc = acc_ref[...]
    o_ref[...] = jnp.where(lane < D_HEAD, acc[:tq], acc[tq:]).astype(o_ref.dtype)


def _sb_prompt(q, kv, n, t, tq):
    pairs = SB_WIDTH // LANES
    nq = t // tq
    return pl.pallas_call(
        functools.partial(_sb_prompt_kernel, tq=tq),
        grid=(n, pairs, nq),
        in_specs=[pl.BlockSpec((tq, LANES), lambda b, p, i: (b * nq + i, p)),
                  pl.BlockSpec((t, LANES), lambda b, p, i: (b, p)),
                  pl.BlockSpec((t, LANES), lambda b, p, i: (b, pairs + p))],
        out_specs=pl.BlockSpec((tq, LANES), lambda b, p, i: (b * nq + i, p)),
        out_shape=jax.ShapeDtypeStruct((n * t, SB_WIDTH), BF16),
        scratch_shapes=[pltpu.VMEM((2 * tq, LANES), F32), pltpu.VMEM((2 * tq, LANES), F32)],
        compiler_params=_params("arbitrary", "arbitrary", "arbitrary"),
        name="sb_prompt",
    )(q, kv, kv)


def _nsa_prompt_kernel(slopes_ref, q_ref, cmp_ref, cw_ref, ksel_ref, vsel_ref, kwin_ref, vwin_ref,
                       g0_ref, g1_ref, g2_ref, o_ref, ck_ref, s_ref, mrun_ref, acc_ref, osel_ref, *, tq, t_len):
    g = pl.program_id(1)
    i = pl.program_id(2)
    nsel = t_len // SEL_BLOCK
    rows = NSA_REP * tq
    lane = lax.broadcasted_iota(jnp.int32, (1, LANES), 1)
    low = lane < D_HEAD

    @pl.when(i == 0)
    def _compress():
        xr = cmp_ref[...].reshape(nsel, SEL_BLOCK, KV_WIDTH)
        cw = cw_ref[...][None]
        own = jnp.where(low, 0, 1) == g
        halves = (jnp.sum(xr[:, :CMP_BLOCK, :] * cw, axis=1), jnp.sum(xr[:, CMP_BLOCK:, :] * cw, axis=1))
        for e, c in enumerate(halves):
            for kv in range(2):
                part = c[:, kv * LANES:(kv + 1) * LANES]
                ck_ref[2 * kv + e] = jnp.where(own, part, pltpu.roll(part, D_HEAD, 1)).astype(BF16)

    q = q_ref[...]
    zero = jnp.zeros((tq, LANES), q.dtype)
    pieces = []
    for b in range(NSA_REP // 2):
        qb = q[:, b * LANES:(b + 1) * LANES]
        pieces += [jnp.where(low, qb, zero), jnp.where(low, zero, qb)]
    qs = jnp.concatenate(pieces, axis=0)
    slope = [slopes_ref[g * NSA_REP + r] for r in range(NSA_REP)]
    slope_col = jnp.concatenate([jnp.full((tq, 1), slope[r], F32) for r in range(NSA_REP)], axis=0)
    t_col = i * tq + lax.rem(lax.broadcasted_iota(jnp.int32, (rows, 1), 0), tq)

    kde, kdo, vde, vdo = ck_ref[0], ck_ref[1], ck_ref[2], ck_ref[3]

    jl = lax.broadcasted_iota(jnp.int32, (1, nsel), 1)
    end_e = jl * SEL_BLOCK + (CMP_BLOCK - 1)
    end_o = jl * SEL_BLOCK + (SEL_BLOCK - 1)
    s_e = _nt(qs, kde) - slope_col * (t_col - end_e).astype(F32)
    s_o = _nt(qs, kdo) - slope_col * (t_col - end_o).astype(F32)
    ok_e = end_e <= t_col
    ok_o = end_o <= t_col
    mx = jnp.maximum(jnp.max(jnp.where(ok_e, s_e, NEG), axis=1, keepdims=True),
                     jnp.max(jnp.where(ok_o, s_o, NEG), axis=1, keepdims=True))
    mx = jnp.where(mx > 0.5 * NEG, mx, 0.0)
    e_e = jnp.where(ok_e, jnp.exp(s_e - mx), 0.0)
    e_o = jnp.where(ok_o, jnp.exp(s_o - mx), 0.0)
    den = jnp.maximum(jnp.sum(e_e, axis=1, keepdims=True) + jnp.sum(e_o, axis=1, keepdims=True), 1e-30)
    o_cmp = _mm((e_e / den).astype(BF16), vde) + _mm((e_o / den).astype(BF16), vdo)

    jr = lax.broadcasted_iota(jnp.int32, (nsel, tq), 0)
    tl = i * tq + lax.broadcasted_iota(jnp.int32, (nsel, tq), 1)
    end_et = jr * SEL_BLOCK + (CMP_BLOCK - 1)
    end_ot = jr * SEL_BLOCK + (SEL_BLOCK - 1)
    ok_et = end_et <= tl
    ok_ot = end_ot <= tl
    st_e = _nt(kde, qs)
    st_o = _nt(kdo, qs)
    score = jnp.zeros((nsel, tq), F32)
    for r in range(NSA_REP):
        a_e = st_e[:, r * tq:(r + 1) * tq] - slope[r] * (tl - end_et).astype(F32)
        a_o = st_o[:, r * tq:(r + 1) * tq] - slope[r] * (tl - end_ot).astype(F32)
        mt = jnp.maximum(jnp.max(jnp.where(ok_et, a_e, NEG), axis=0, keepdims=True),
                         jnp.max(jnp.where(ok_ot, a_o, NEG), axis=0, keepdims=True))
        mt = jnp.where(mt > 0.5 * NEG, mt, 0.0)
        x_e = jnp.where(ok_et, jnp.exp(a_e - mt), 0.0)
        x_o = jnp.where(ok_ot, jnp.exp(a_o - mt), 0.0)
        dt = jnp.maximum(jnp.sum(x_e, axis=0, keepdims=True) + jnp.sum(x_o, axis=0, keepdims=True), 1e-30)
        score = score + (x_e / dt + x_o / dt)
    cur = tl // SEL_BLOCK
    forced = (jr == 0) | (jr == cur) | (jr == cur - 1)
    score = jnp.where(forced, FORCE_SCORE, jnp.where(jr * SEL_BLOCK <= tl, score, -1.0))
    rank = jnp.zeros((nsel, tq), jnp.int32)
    for j2 in range(nsel):
        other = score[j2:j2 + 1, :]
        ahead = (other > score) | ((other == score) & (jr > j2))
        rank = rank + jnp.where(ahead, 1, 0)
    sel_t = jnp.where(rank < min(TOP_N, nsel), 1.0, 0.0).astype(BF16)
    eye_r = lax.broadcasted_iota(jnp.int32, (tq, tq), 0)
    eye_c = lax.broadcasted_iota(jnp.int32, (tq, tq), 1)
    eye = jnp.where(eye_r == eye_c, 1.0, 0.0).astype(BF16)
    sel = _nt(eye, sel_t).astype(BF16)

    t0 = i * tq
    lane_g = lax.broadcasted_iota(jnp.int32, (1, KEY_GROUP), 1)
    row_minus_lane = (lax.broadcasted_iota(jnp.int32, (tq, KEY_GROUP), 0)
                      - lax.broadcasted_iota(jnp.int32, (tq, KEY_GROUP), 1))
    ones_v = jnp.ones((KEY_GROUP, LANES), BF16)
    g_hi = (t0 + tq + KEY_GROUP - 1) // KEY_GROUP

    def for_groups(g_lo, body):
        n = g_hi - g_lo

        def pair(jj, c):
            body(g_lo + 2 * jj)
            body(g_lo + 2 * jj + 1)
            return c

        lax.fori_loop(0, n // 2, pair, 0)

        @pl.when(lax.rem(n, 2) == 1)
        def _():
            body(g_hi - 1)

    def sweep(k_ref_, v_ref_, g_lo, mask_bias):
        mrun_ref[...] = jnp.full_like(mrun_ref, NEG)

        def pass1(gi):
            off = pl.multiple_of(gi * KEY_GROUP, KEY_GROUP)
            add = mask_bias(gi, row_minus_lane + (t0 - off))
            k_rel = (off - t0 + lane_g).astype(F32)
            raw = _nt(qs, k_ref_[pl.ds(off, KEY_GROUP), :])
            s = jnp.concatenate([raw[r * tq:(r + 1) * tq] + (add + slope[r] * k_rel) for r in range(NSA_REP)],
                                axis=0)
            s_ref[gi - g_lo] = s
            mrun_ref[...] = jnp.maximum(mrun_ref[...], s)

        for_groups(g_lo, pass1)
        mrun_ref[...] = jnp.broadcast_to(jnp.max(mrun_ref[...], axis=1, keepdims=True), mrun_ref.shape)
        acc_ref[...] = jnp.zeros_like(acc_ref)

        def pass2(gi):
            off = pl.multiple_of(gi * KEY_GROUP, KEY_GROUP)
            p = jnp.exp(s_ref[gi - g_lo] - mrun_ref[...]).astype(BF16)
            acc_ref[...] += _mm(p, jnp.concatenate([v_ref_[pl.ds(off, KEY_GROUP), :], ones_v], axis=1))

        for_groups(g_lo, pass2)
        acc = acc_ref[...]
        return acc[:, :LANES] / jnp.maximum(acc[:, LANES:], 1e-30)

    per_group = KEY_GROUP // SEL_BLOCK
    br = lax.broadcasted_iota(jnp.int32, (nsel, KEY_GROUP), 0)
    bc = lax.broadcasted_iota(jnp.int32, (nsel, KEY_GROUP), 1) // SEL_BLOCK
    not_sel = ((sel.astype(F32) - 1.0) * (-NEG)).astype(BF16)

    def sel_bias(gi, dist):
        expand = jnp.where(br == gi * per_group + bc, 1.0, 0.0).astype(BF16)
        return jnp.where(dist >= 0, _mm(not_sel, expand), NEG)

    osel_ref[...] = sweep(ksel_ref, vsel_ref, 0, sel_bias)

    def win_bias(gi, dist):
        return jnp.where((dist >= 0) & (dist < WINDOW), 0.0, NEG)

    o_win = sweep(kwin_ref, vwin_ref, jnp.maximum(t0 - (WINDOW - 1), 0) // KEY_GROUP, win_bias)
    o_sel = osel_ref[...]

    for b in range(NSA_REP // 2):
        lo_rows = slice(2 * b * tq, (2 * b + 1) * tq)
        hi_rows = slice((2 * b + 1) * tq, (2 * b + 2) * tq)
        cols = slice(b * LANES, (b + 1) * LANES)
        pick = lambda o: jnp.where(low, o[lo_rows], o[hi_rows])
        merged = (g0_ref[:, cols].astype(F32) * pick(o_cmp) + g1_ref[:, cols].astype(F32) * pick(o_sel)
                  + g2_ref[:, cols].astype(F32) * pick(o_win))
        o_ref[:, cols] = merged.astype(o_ref.dtype)


def _nsa_prompt(slopes, q, cmp_kv, cw, sel_dup, win_dup, gates, n, t, tq):
    nq = t // tq
    gw = NSA_REP * D_HEAD
    row = lambda b, g, i: b * nq + i
    seq = lambda c: pl.BlockSpec((t, LANES), lambda b, g, i, c=c: (b, c * NSA_KV_HEADS + g))
    gate = lambda br: pl.BlockSpec((tq, gw), lambda b, g, i, br=br: (row(b, g, i), br * NSA_KV_HEADS + g))
    nsel = t // SEL_BLOCK
    rows = NSA_REP * tq
    return pl.pallas_call(
        functools.partial(_nsa_prompt_kernel, tq=tq, t_len=t),
        grid=(n, NSA_KV_HEADS, nq),
        in_specs=[pl.BlockSpec(memory_space=pltpu.SMEM),
                  pl.BlockSpec((tq, gw), lambda b, g, i: (row(b, g, i), g)),
                  pl.BlockSpec((t, KV_WIDTH), lambda b, g, i: (b, 0)),
                  pl.BlockSpec((CMP_BLOCK, KV_WIDTH), lambda b, g, i: (0, 0)),
                  seq(0), seq(1), seq(0), seq(1), gate(0), gate(1), gate(2)],
        out_specs=pl.BlockSpec((tq, gw), lambda b, g, i: (row(b, g, i), g)),
        out_shape=jax.ShapeDtypeStruct((n * t, NSA_WIDTH), BF16),
        scratch_shapes=[pltpu.VMEM((4, nsel, LANES), BF16), pltpu.VMEM((t // KEY_GROUP, rows, KEY_GROUP), F32),
                        pltpu.VMEM((rows, KEY_GROUP), F32), pltpu.VMEM((rows, 2 * LANES), F32),
                        pltpu.VMEM((rows, LANES), F32)],
        compiler_params=_params("arbitrary", "arbitrary", "arbitrary"),
        name="nsa_prompt",
    )(slopes, q, cmp_kv, cw, sel_dup, sel_dup, win_dup, win_dup, gates, gates, gates)


def _sb_sample_kernel(pt_ref, q_ref, new_ref, *rest, n_step, ts):
    page_refs, o_ref, acc_ref, carry_ref = rest[:n_step], rest[n_step], rest[n_step + 1], rest[n_step + 2]
    jj = pl.program_id(1)
    rows = ts * SB_HEADS
    rr = lax.broadcasted_iota(jnp.int32, (rows, SB_WIDTH), 0)
    cc = lax.broadcasted_iota(jnp.int32, (rows, SB_WIDTH), 1)
    own_head = (cc // D_HEAD) == lax.rem(rr, SB_HEADS)
    q = q_ref[...]
    qrows = jnp.where(own_head, q, jnp.zeros_like(q))
    later = _later_and_rowsum()

    @pl.when(jj == 0)
    def _first():
        acc_ref[...] = jnp.zeros_like(acc_ref)
        carry_ref[...] = jnp.zeros_like(carry_ref)
        new = new_ref[...]
        s_key = lax.broadcasted_iota(jnp.int32, (rows, PAGE_SIZE), 1)
        t_qry = lax.broadcasted_iota(jnp.int32, (rows, PAGE_SIZE), 0) // SB_HEADS
        _sb_group(_nt(qrows, new[:, :SB_WIDTH]), s_key < t_qry, lambda a: _mm(a, new[:, SB_WIDTH:]),
                  later, acc_ref, carry_ref, latest_first=True)

    k_t = jnp.concatenate([p[:SB_WIDTH, :] for p in page_refs], axis=1).astype(BF16)
    v_t = jnp.concatenate([p[SB_WIDTH:, :] for p in page_refs], axis=1).astype(BF16)
    _sb_group(_mm(qrows, k_t), None, lambda a: _nt(a, v_t), later, acc_ref, carry_ref, latest_first=True)

    @pl.when(jj == pl.num_programs(1) - 1)
    def _last():
        own = jnp.where(own_head, acc_ref[...], 0.0)
        o_ref[...] = jnp.sum(own.reshape(ts, SB_HEADS, SB_WIDTH), axis=1).astype(o_ref.dtype)


def _sb_sample(page_table, q_rep, kv_new, cache, n_step):
    ns, rows, _ = q_rep.shape
    ts = rows // SB_HEADS
    n_pages = page_table.shape[1]
    steps = n_pages // n_step

    def page_spec(u):
        def index(b, jj, pt):
            return (pt[b * n_pages + (n_pages - 1 - (jj * n_step + u))], 0, 0)
        return pl.BlockSpec((None, 2 * SB_WIDTH, PAGE_SIZE), index)

    grid_spec = pltpu.PrefetchScalarGridSpec(
        num_scalar_prefetch=1,
        grid=(ns, steps),
        in_specs=[pl.BlockSpec((None, rows, SB_WIDTH), lambda b, jj, pt: (b, 0, 0)),
                  pl.BlockSpec((None, PAGE_SIZE, 2 * SB_WIDTH), lambda b, jj, pt: (b, 0, 0))]
                 + [page_spec(u) for u in range(n_step)],
        out_specs=pl.BlockSpec((None, ts, SB_WIDTH), lambda b, jj, pt: (b, 0, 0)),
        scratch_shapes=[pltpu.VMEM((rows, SB_WIDTH), F32), pltpu.VMEM((rows, LANES), F32)])
    return pl.pallas_call(
        functools.partial(_sb_sample_kernel, n_step=n_step, ts=ts),
        grid_spec=grid_spec,
        out_shape=jax.ShapeDtypeStruct((ns, ts, SB_WIDTH), BF16),
        compiler_params=_params("arbitrary", "arbitrary"),
        name="sb_sample",
    )(page_table.reshape(-1), q_rep, kv_new, *([cache] * n_step))


def _pool_matrices(n_pages, n_step):
    steps = n_pages // n_step
    nps = n_pages * PAGE_SIZE // SEL_BLOCK
    key = jnp.arange(steps * n_step * PAGE_SIZE)
    col = (key // CMP_BLOCK % 2) * nps + key // SEL_BLOCK
    return jax.nn.one_hot(col, 2 * nps, dtype=BF16).reshape(steps, n_step * PAGE_SIZE, 2 * nps)


def _cmp_sample_kernel(pt_ref, slopes_ref, q_ref, cw_ref, pool_ref, *rest, n_step, ts, past):
    page_refs = rest[:n_step]
    o_ref, idx_ref, ck_ref = rest[n_step:]
    jj = pl.program_id(1)
    nps = past // SEL_BLOCK
    rows = q_ref.shape[1]
    n_cols = NSA_KV_HEADS * ts

    @pl.when(jj == 0)
    def _zero():
        ck_ref[...] = jnp.zeros_like(ck_ref)

    cw = cw_ref[...]
    weighted = jnp.concatenate([r[...] * cw for r in page_refs], axis=1)
    pooled = _mm(jnp.concatenate(_split_bf16(weighted), axis=0), pool_ref[jj])
    ck_ref[...] += pooled[:KV_WIDTH] + pooled[KV_WIDTH:]

    @pl.when(jj == pl.num_programs(1) - 1)
    def _attend():
        ck = ck_ref[...]
        ke_t, ko_t = ck[:LANES, :nps].astype(BF16), ck[:LANES, nps:].astype(BF16)
        ve_t, vo_t = ck[LANES:, :nps].astype(BF16), ck[LANES:, nps:].astype(BF16)
        jl = lax.broadcasted_iota(jnp.int32, (rows, nps), 1)
        rl = lax.broadcasted_iota(jnp.int32, (rows, nps), 0)
        pos_n = past + lax.rem(rl, ts)
        d_e = pos_n - (jl * SEL_BLOCK + CMP_BLOCK - 1)
        d_o = pos_n - (jl * SEL_BLOCK + SEL_BLOCK - 1)
        p_sum = jnp.zeros((rows, nps), F32)
        for r in range(NSA_REP):
            q = q_ref[r]
            sl = [slopes_ref[g * NSA_REP + r] for g in range(NSA_KV_HEADS)]
            slope = jnp.where(rl // ts == 0, sl[0], sl[1])
            s_e = _mm(q, ke_t) - slope * d_e.astype(F32)
            s_o = _mm(q, ko_t) - slope * d_o.astype(F32)
            mx = jnp.maximum(jnp.max(jnp.where(d_e >= 0, s_e, NEG), axis=1, keepdims=True),
                             jnp.max(jnp.where(d_o >= 0, s_o, NEG), axis=1, keepdims=True))
            mx = jnp.where(mx > 0.5 * NEG, mx, 0.0)
            e_e = jnp.where(d_e >= 0, jnp.exp(s_e - mx), 0.0)
            e_o = jnp.where(d_o >= 0, jnp.exp(s_o - mx), 0.0)
            den = jnp.maximum(jnp.sum(e_e, axis=1, keepdims=True) + jnp.sum(e_o, axis=1, keepdims=True), 1e-30)
            p_e, p_o = e_e / den, e_o / den
            o_ref[r] = _nt(p_e.astype(BF16), ve_t) + _nt(p_o.astype(BF16), vo_t)
            p_sum = p_sum + (p_e + p_o)
        cur = pos_n // SEL_BLOCK
        forced = (jl == 0) | (jl == cur) | (jl == cur - 1)
        score = jnp.where(forced, FORCE_SCORE, jnp.where(jl * SEL_BLOCK <= pos_n, p_sum, -1.0))
        pos_1 = past + lax.rem(lax.broadcasted_iota(jnp.int32, (rows, 1), 0), ts)
        cur_1 = pos_1 // SEL_BLOCK
        forced_1 = (cur_1 == nps) | (cur_1 - 1 == nps)
        tail = jnp.where(forced_1, FORCE_SCORE, jnp.where(nps * SEL_BLOCK <= pos_1, 0.0, -1.0))
        er = lax.broadcasted_iota(jnp.int32, (nps, nps), 0)
        ec = lax.broadcasted_iota(jnp.int32, (nps, nps), 1)
        eye = jnp.where(er == ec, 1.0, 0.0).astype(BF16)
        score_t = sum(_nt(eye, piece) for piece in _split3_bf16(score))
        k_f = lax.broadcasted_iota(jnp.int32, (TOP_N, nps), 0).astype(F32)
        j_f = lax.broadcasted_iota(jnp.int32, (TOP_N, nps), 1).astype(F32)
        k_col = lax.broadcasted_iota(jnp.int32, (TOP_N, 1), 0).astype(F32)
        c_lane = lax.broadcasted_iota(jnp.int32, (TOP_N, rows), 1)
        picked = jnp.zeros((TOP_N, rows), F32)
        for c in range(n_cols):
            other = score_t[:, c:c + 1]
            mine = score[c:c + 1, :]
            tail_c = tail[c:c + 1, :]
            ahead = (other > mine) | ((other == mine) & (er < ec))
            rank = jnp.sum(jnp.where(ahead, 1.0, 0.0), axis=0, keepdims=True) + jnp.where(tail_c > mine, 1.0, 0.0)
            rank_tail = jnp.sum(jnp.where(mine >= tail_c, 1.0, 0.0), axis=1, keepdims=True)
            hit = jnp.sum(jnp.where(rank == k_f, j_f, 0.0), axis=1, keepdims=True)
            hit = hit + jnp.where(rank_tail == k_col, float(nps), 0.0)
            picked = jnp.where(c_lane == c, hit, picked)
        idx_ref[...] = picked.astype(jnp.int32)


def _cmp_sample(page_table, slopes, q, cw_t, cache, n_step, ts):
    ns, _, rows, _ = q.shape
    n_pages = page_table.shape[1]
    steps = n_pages // n_step
    past = n_pages * PAGE_SIZE
    nps = past // SEL_BLOCK
    pool = _pool_matrices(n_pages, n_step)

    def page_spec(u):
        return pl.BlockSpec((None, KV_WIDTH, PAGE_SIZE),
                            lambda b, jj, pt: (pt[b * n_pages + jj * n_step + u], 0, 0))

    grid_spec = pltpu.PrefetchScalarGridSpec(
        num_scalar_prefetch=1,
        grid=(ns, steps),
        in_specs=[pl.BlockSpec(memory_space=pltpu.SMEM),
                  pl.BlockSpec((None, NSA_REP, rows, LANES), lambda b, jj, pt: (b, 0, 0, 0)),
                  pl.BlockSpec(cw_t.shape, lambda b, jj, pt: (0, 0)),
                  pl.BlockSpec(pool.shape, lambda b, jj, pt: (0, 0, 0))]
                 + [page_spec(u) for u in range(n_step)],
        out_specs=[pl.BlockSpec((None, NSA_REP, rows, LANES), lambda b, jj, pt: (b, 0, 0, 0)),
                   pl.BlockSpec((None, TOP_N, rows), lambda b, jj, pt: (b, 0, 0))],
        scratch_shapes=[pltpu.VMEM((KV_WIDTH, 2 * nps), F32)])
    return pl.pallas_call(
        functools.partial(_cmp_sample_kernel, n_step=n_step, ts=ts, past=past),
        grid_spec=grid_spec,
        out_shape=[jax.ShapeDtypeStruct((ns, NSA_REP, rows, LANES), F32),
                   jax.ShapeDtypeStruct((ns, TOP_N, rows), jnp.int32)],
        compiler_params=_params("arbitrary", "arbitrary"),
        name="cmp_sample",
    )(page_table.reshape(-1), slopes, q, cw_t, pool, *([cache] * n_step))


def _sel_sample_kernel(page_ref, idx_ref, slopes_ref, q_ref, new_ref, *rest, ts, past, cols):
    kv_refs, o_ref = rest[:TOP_N], rest[TOP_N]
    b, g, t = pl.program_id(0), pl.program_id(1), pl.program_id(2)
    nps = past // SEL_BLOCK
    per_page = PAGE_SIZE // SEL_BLOCK
    rows = q_ref.shape[0]
    width = TOP_N * PAGE_SIZE
    pos = past + t
    q = q_ref[...]
    rr = lax.broadcasted_iota(jnp.int32, (rows, 1), 0)
    slope = jnp.zeros((rows, 1), F32)
    for r in range(NSA_REP):
        slope = jnp.where(rr == r, slopes_ref[g * NSA_REP + r], slope)
    its = [idx_ref[(b * TOP_N + k) * cols + g * ts + t] for k in range(TOP_N)]
    ll = lax.broadcasted_iota(jnp.int32, (1, width), 1)
    block_of = jnp.zeros((1, width), jnp.int32)
    for k in range(TOP_N):
        block_of = jnp.where(ll // PAGE_SIZE == k, its[k], block_of)
    in_page = lax.rem(ll, PAGE_SIZE)
    k_pos = block_of * SEL_BLOCK + lax.rem(in_page, SEL_BLOCK)
    ok = (block_of < nps) & (in_page // SEL_BLOCK == lax.rem(block_of, per_page)) & (k_pos <= pos)
    k_t = jnp.concatenate([r[0] for r in kv_refs], axis=1).astype(BF16)
    v_t = jnp.concatenate([r[1] for r in kv_refs], axis=1).astype(BF16)
    s = _mm(q, k_t) - slope * (pos - k_pos).astype(F32)
    n_sel_new = jnp.where(its[0] >= nps, 1, 0)
    for k in range(1, TOP_N):
        n_sel_new = n_sel_new + jnp.where(its[k] >= nps, 1, 0)
    limit = jnp.where(n_sel_new > 0, pos, past - 1)
    n_new = new_ref.shape[1]
    kn_pos = past + lax.broadcasted_iota(jnp.int32, (1, n_new), 1)
    ok_n = kn_pos <= limit
    k_n = new_ref[0].astype(BF16)
    v_n = new_ref[1].astype(BF16)
    s_n = _nt(q, k_n) - slope * (pos - kn_pos).astype(F32)
    mx = jnp.maximum(jnp.max(jnp.where(ok, s, NEG), axis=1, keepdims=True),
                     jnp.max(jnp.where(ok_n, s_n, NEG), axis=1, keepdims=True))
    mx = jnp.where(mx > 0.5 * NEG, mx, 0.0)
    e = jnp.where(ok, jnp.exp(s - mx), 0.0)
    e_n = jnp.where(ok_n, jnp.exp(s_n - mx), 0.0)
    den = jnp.maximum(jnp.sum(e, axis=1, keepdims=True) + jnp.sum(e_n, axis=1, keepdims=True), 1e-30)
    o_ref[...] = _nt((e / den).astype(BF16), v_t) + _mm((e_n / den).astype(BF16), v_n)


def _sel_sample(page_table, idx, slopes, q, kv_new, cache, ts):
    ns = q.shape[0]
    n_pages = page_table.shape[1]
    past = n_pages * PAGE_SIZE
    nps = past // SEL_BLOCK
    per_page = PAGE_SIZE // SEL_BLOCK
    cols = idx.shape[2]
    rows = q.shape[3]
    pages = jnp.take_along_axis(page_table[:, None, :], jnp.minimum(idx, nps - 1) // per_page, axis=2)

    def slab_spec(k):
        def index(b, g, t, pg, ix):
            return (pg[(b * TOP_N + k) * cols + g * ts + t], 0, g, 0, 0)
        return pl.BlockSpec((None, 2, None, D_HEAD, PAGE_SIZE), index)

    grid_spec = pltpu.PrefetchScalarGridSpec(
        num_scalar_prefetch=2,
        grid=(ns, NSA_KV_HEADS, ts),
        in_specs=[pl.BlockSpec(memory_space=pltpu.SMEM),
                  pl.BlockSpec((None, None, None, rows, D_HEAD), lambda b, g, t, pg, ix: (b, g, t, 0, 0)),
                  pl.BlockSpec((None, None) + kv_new.shape[2:], lambda b, g, t, pg, ix: (b, g, 0, 0, 0))]
                 + [slab_spec(k) for k in range(TOP_N)],
        out_specs=pl.BlockSpec((None, None, None, rows, D_HEAD), lambda b, g, t, pg, ix: (b, g, t, 0, 0)))
    return pl.pallas_call(
        functools.partial(_sel_sample_kernel, ts=ts, past=past, cols=cols),
        grid_spec=grid_spec,
        out_shape=jax.ShapeDtypeStruct((ns, NSA_KV_HEADS, ts, rows, D_HEAD), F32),
        compiler_params=_params("arbitrary", "arbitrary", "arbitrary"),
        name="sel_sample",
    )(pages.reshape(-1), idx.reshape(-1), slopes, q, kv_new, *([cache] * TOP_N))


def _win_sample_kernel(slopes_ref, q_ref, buf_ref, new_ref, o_ref, *, ts, past):
    rows = NSA_HEADS * ts
    w = buf_ref.shape[1]
    rr = lax.broadcasted_iota(jnp.int32, (rows, 1), 0)
    slope = jnp.zeros((rows, 1), F32)
    for h in range(NSA_HEADS):
        slope = jnp.where(rr // ts == h, slopes_ref[h], slope)
    pos = past + lax.rem(rr, ts)
    q = q_ref[...]
    new = new_ref[...]
    n_new = new.shape[0]
    kb_pos = past - w + lax.broadcasted_iota(jnp.int32, (1, w), 1)
    kn_pos = past + lax.broadcasted_iota(jnp.int32, (1, n_new), 1)
    d_b = pos - kb_pos
    d_n = pos - kn_pos
    ok_b = (d_b >= 0) & (d_b < WINDOW) & (kb_pos >= 0)
    ok_n = (d_n >= 0) & (d_n < WINDOW)
    s_b = _mm(q, buf_ref[:LANES, :].astype(BF16)) - slope * d_b.astype(F32)
    s_n = _nt(q, new[:, :LANES].astype(BF16)) - slope * d_n.astype(F32)
    mx = jnp.maximum(jnp.max(jnp.where(ok_b, s_b, NEG), axis=1, keepdims=True),
                     jnp.max(jnp.where(ok_n, s_n, NEG), axis=1, keepdims=True))
    mx = jnp.where(mx > 0.5 * NEG, mx, 0.0)
    e_b = jnp.where(ok_b, jnp.exp(s_b - mx), 0.0)
    e_n = jnp.where(ok_n, jnp.exp(s_n - mx), 0.0)
    den = jnp.maximum(jnp.sum(e_b, axis=1, keepdims=True) + jnp.sum(e_n, axis=1, keepdims=True), 1e-30)
    o_ref[...] = (_nt((e_b / den).astype(BF16), buf_ref[LANES:, :].astype(BF16))
                  + _mm((e_n / den).astype(BF16), new[:, LANES:].astype(BF16)))


def _win_sample(slopes, q, win_buf_t, kv_new, ts, past):
    ns, rows, _ = q.shape
    return pl.pallas_call(
        functools.partial(_win_sample_kernel, ts=ts, past=past),
        grid=(ns,),
        in_specs=[pl.BlockSpec(memory_space=pltpu.SMEM),
                  pl.BlockSpec((None, rows, LANES), lambda b: (b, 0, 0)),
                  pl.BlockSpec((None,) + win_buf_t.shape[1:], lambda b: (b, 0, 0)),
                  pl.BlockSpec((None,) + kv_new.shape[1:], lambda b: (b, 0, 0))],
        out_specs=pl.BlockSpec((None, rows, LANES), lambda b: (b, 0, 0)),
        out_shape=jax.ShapeDtypeStruct((ns, rows, LANES), F32),
        compiler_params=_params("arbitrary"),
        name="win_sample",
    )(slopes, q, win_buf_t, kv_new)


def _merge_sample_kernel(cmp_ref, sel_ref, win_ref, gate_ref, o_ref):
    lane = lax.broadcasted_iota(jnp.int32, (1, LANES), 1)
    low = lane < D_HEAD
    for g in range(NSA_KV_HEADS):
        for b in range(NSA_REP // 2):
            col = g * NSA_REP * D_HEAD + b * LANES
            total = None
            for br, ref in enumerate((cmp_ref, sel_ref, win_ref)):
                first, second = ref[g, 2 * b], ref[g, 2 * b + 1]
                if g == 0:
                    val = jnp.where(low, first, pltpu.roll(second, D_HEAD, 1))
                else:
                    val = jnp.where(low, pltpu.roll(first, D_HEAD, 1), second)
                gate = gate_ref[:, br * NSA_WIDTH + col: br * NSA_WIDTH + col + LANES].astype(F32)
                total = gate * val if total is None else total + gate * val
            o_ref[:, col:col + LANES] = total.astype(o_ref.dtype)


def _merge_sample(o_cmp, o_sel, o_win, gates):
    rows = gates.shape[0]
    full = lambda a: pl.BlockSpec(a.shape, lambda i: (0,) * a.ndim)
    return pl.pallas_call(
        _merge_sample_kernel,
        grid=(1,),
        in_specs=[full(o_cmp), full(o_sel), full(o_win), full(gates)],
        out_specs=pl.BlockSpec((rows, NSA_WIDTH), lambda i: (0, 0)),
        out_shape=jax.ShapeDtypeStruct((rows, NSA_WIDTH), BF16),
        compiler_params=_params("arbitrary"),
        name="merge_sample",
    )(o_cmp, o_sel, o_win, gates)


def _dup_kv_cols(w):
    d = w.shape[0]
    return jnp.repeat(w.reshape(d, 2 * NSA_KV_HEADS, 1, D_HEAD), 2, axis=2).reshape(d, 2 * KV_WIDTH)


def _gate_cols(w):
    d = w.shape[0]
    per = w.reshape(d, NSA_HEADS, N_NSA_BRANCHES).transpose(0, 2, 1)
    return jnp.repeat(per[..., None], D_HEAD, axis=3).reshape(d, N_NSA_BRANCHES * NSA_WIDTH)


def _expand_q_cols(w):
    d = w.shape[0]
    place = jax.nn.one_hot(jnp.arange(NSA_HEADS) // NSA_REP, NSA_KV_HEADS, dtype=w.dtype)
    return (w.reshape(d, NSA_HEADS, 1, D_HEAD) * place[None, :, :, None]).reshape(d, NSA_HEADS * LANES)


def _pad_rows(a, rows):
    return jnp.pad(a, [(0, 0)] * (a.ndim - 2) + [(0, rows - a.shape[-2]), (0, 0)])


def _keys_on_lanes(a):
    b, s = a.shape[:2]
    return jnp.transpose(a, (0, 2, 3, 4, 1)).reshape(b, -1, s)


def _groups(spec):
    groups, c = [], 0
    for width, act, dts in spec:
        groups.append((c, width, act, dts))
        c += width
    return tuple(groups)


def _prompt_layer(x, mods, lw, slopes, n, t, final):
    (w_in, g_mix, g_mlp, g_final, cmp_w, w_br_sb, w_br_nsa, w_out, w_up, w_down) = lw
    sh1, sc1, gt1, sh2, sc2, gt2 = mods
    tm = min(256, t)
    w_cat = jnp.concatenate([
        w_in[:, _C_SBQ:_C_SBKV] * ATTN_SCALE, w_in[:, _C_SBKV:_C_NSQ], w_in[:, _C_NSQ:_C_CMP] * ATTN_SCALE,
        w_in[:, _C_CMP:_C_GATE], _dup_kv_cols(w_in[:, _C_SEL:_C_WIN]), _dup_kv_cols(w_in[:, _C_WIN:_C_GATE]),
        _gate_cols(w_in[:, _C_GATE:_C_GSB])], axis=1).astype(BF16)
    groups = _groups(((SB_WIDTH, None, (BF16,)), (2 * SB_WIDTH, None, (F32, BF16)), (NSA_WIDTH, None, (BF16,)),
                      (KV_WIDTH, None, (F32,)), (KV_WIDTH, None, (F32,)), (KV_WIDTH, None, (F32,)),
                      (2 * KV_WIDTH, None, (BF16,)), (2 * KV_WIDTH, None, (BF16,)),
                      (N_NSA_BRANCHES * NSA_WIDTH, "sigmoid", (BF16,))))
    (sbq, sbkv, sbkv_b, nsq, cmp_kv, sel_kv, win_kv, sel_dup, win_dup, gates) = _project(
        x, g_mix, sc1, sh1, w_cat, groups, tm, t)
    o_sb = _sb_prompt(sbq, sbkv_b, n, t, min(256, t))
    cw = jnp.repeat(cmp_w.reshape(CMP_BLOCK, 2 * NSA_KV_HEADS), D_HEAD, axis=1)
    o_nsa = _nsa_prompt(slopes, nsq, cmp_kv, cw, sel_dup, win_dup, gates, n, t, LANES)
    x1 = _mix(x, g_mix, sc1, sh1, gt1, o_sb, o_nsa, w_in[:, _C_GSB:].astype(BF16), w_br_sb.astype(BF16),
              w_br_nsa.astype(BF16), w_out.astype(BF16), tm, t)
    x2 = _mlp(x1, g_mlp, sc2, sh2, gt2, g_final, w_up.astype(BF16), w_down.astype(BF16), tm, t, final)
    return x2, (sbkv, cmp_kv, sel_kv, win_kv)


def _sample_mixers(sbq, sbkv_b, nsqx, sel_kv, win_kv, gates, cmp_w, slopes, caches, page_table, ns, ts):
    cache_sb, cache_cmp, cache_sel, win_buf = caches
    m = ns * ts
    n_pages = page_table.shape[1]
    past = n_pages * PAGE_SIZE
    n_step = math.gcd(n_pages, 8)
    g_, r_ = NSA_KV_HEADS, NSA_REP
    q_rep = jnp.broadcast_to(sbq.reshape(ns, ts, 1, SB_WIDTH), (ns, ts, SB_HEADS, SB_WIDTH))
    q_rep = q_rep.reshape(ns, ts * SB_HEADS, SB_WIDTH)
    kv_new = _pad_rows(sbkv_b.reshape(ns, ts, 2 * SB_WIDTH), PAGE_SIZE)
    o_sb = _sb_sample(page_table, q_rep, kv_new, _keys_on_lanes(cache_sb), n_step).reshape(m, SB_WIDTH)

    qx = nsqx.reshape(ns, ts, g_, r_, LANES)
    q_cmp = _pad_rows(qx.transpose(0, 3, 2, 1, 4).reshape(ns, r_, g_ * ts, LANES), 16)
    q_sel = jnp.stack([qx[:, :, g, :, g * D_HEAD:(g + 1) * D_HEAD] for g in range(g_)], axis=1)
    q_sel = _pad_rows(q_sel, 16)
    q_win = qx.transpose(0, 2, 3, 1, 4).reshape(ns, NSA_HEADS * ts, LANES)
    cw_t = jnp.tile(jnp.repeat(cmp_w.reshape(CMP_BLOCK, 2 * g_).T, D_HEAD, axis=0), (1, PAGE_SIZE // CMP_BLOCK))
    o_cmp, idx = _cmp_sample(page_table, slopes, q_cmp, cw_t, _keys_on_lanes(cache_cmp), math.gcd(n_pages, 32), ts)
    kv_rows = lambda a: _pad_rows(a.reshape(ns, ts, KV_WIDTH), 16)
    sel_new = _pad_rows(sel_kv.reshape(ns, ts, 2, g_, D_HEAD).transpose(0, 3, 2, 1, 4), 16)
    sel_slabs = _keys_on_lanes(cache_sel).reshape(-1, 2, g_, D_HEAD, PAGE_SIZE)
    o_sel = _sel_sample(page_table, idx[:, :, :g_ * ts], slopes, q_sel, sel_new, sel_slabs, ts)
    o_win = _win_sample(slopes, q_win, _keys_on_lanes(win_buf), kv_rows(win_kv), ts, past)
    oc = o_cmp[:, :, :g_ * ts].reshape(ns, r_, g_, ts, LANES).transpose(2, 1, 0, 3, 4).reshape(g_, r_, m, LANES)
    os_ = o_sel[:, :, :, :r_].transpose(1, 3, 0, 2, 4).reshape(g_, r_, m, D_HEAD)
    os_ = jnp.stack([jnp.pad(os_[g], ((0, 0), (0, 0), (g * D_HEAD, (g_ - 1 - g) * D_HEAD))) for g in range(g_)])
    ow = o_win.reshape(ns, g_, r_, ts, LANES).transpose(1, 2, 0, 3, 4).reshape(g_, r_, m, LANES)
    return o_sb, _merge_sample(oc, os_, ow, gates)


def _sample_layer(x, mods, lw, slopes, caches, page_table, ns, ts, final):
    (w_in, g_mix, g_mlp, g_final, cmp_w, w_br_sb, w_br_nsa, w_out, w_up, w_down) = lw
    sh1, sc1, gt1, sh2, sc2, gt2 = mods
    tm = sc1.shape[1]
    w_cat = jnp.concatenate([
        w_in[:, _C_SBQ:_C_SBKV] * ATTN_SCALE, w_in[:, _C_SBKV:_C_NSQ],
        _expand_q_cols(w_in[:, _C_NSQ:_C_CMP] * ATTN_SCALE), w_in[:, _C_CMP:_C_GATE],
        _gate_cols(w_in[:, _C_GATE:_C_GSB])], axis=1).astype(BF16)
    groups = _groups(((SB_WIDTH, None, (BF16,)), (2 * SB_WIDTH, None, (F32, BF16)), (NSA_HEADS * LANES, None, (BF16,)),
                      (KV_WIDTH, None, (F32,)), (KV_WIDTH, None, (F32,)), (KV_WIDTH, None, (F32,)),
                      (N_NSA_BRANCHES * NSA_WIDTH, "sigmoid", (BF16,))))
    sbq, sbkv, sbkv_b, nsqx, cmp_kv, sel_kv, win_kv, gates = _project(x, g_mix, sc1, sh1, w_cat, groups, tm, ts)
    o_sb, o_nsa = _sample_mixers(sbq, sbkv_b, nsqx, sel_kv, win_kv, gates, cmp_w, slopes, caches, page_table, ns, ts)
    x1 = _mix(x, g_mix, sc1, sh1, gt1, o_sb, o_nsa, w_in[:, _C_GSB:].astype(BF16), w_br_sb.astype(BF16),
              w_br_nsa.astype(BF16), w_out.astype(BF16), tm, ts)
    x2 = _mlp(x1, g_mlp, sc2, sh2, gt2, g_final, w_up.astype(BF16), w_down.astype(BF16), tm, ts, final)
    return x2, (sbkv, cmp_kv, sel_kv, win_kv)


def kernel(x_prompt, x_sample, cache_sb_kv, cache_cmp_kv, cache_sel_kv, state_win_kv, page_table, c_prompt, c_sample,
           w_ada, b_ada, g_mix, g_mlp, g_final, w_in, cmp_w, w_br_sb, w_br_nsa, w_out, w_up, w_down):
    depth = w_in.shape[0]
    n, t, d = x_prompt.shape
    ns, ts, _ = x_sample.shape
    slopes = jnp.exp2(-8.0 * (jnp.arange(NSA_HEADS, dtype=F32) + 1.0) / NSA_HEADS)
    xp = x_prompt.reshape(n * t, d)
    xs = x_sample.reshape(ns * ts, d)
    ms = ns * ts
    tms = min(LANES, ms)
    c_all = jnp.concatenate([c_prompt, c_sample], axis=0)
    st_p = ([], [], [], [])
    st_s = ([], [], [], [])
    for l in range(depth):
        final = l == depth - 1
        lw = (w_in[l], g_mix[l].reshape(1, d), g_mlp[l].reshape(1, d), g_final.reshape(1, d), cmp_w[l],
              w_br_sb[l], w_br_nsa[l], w_out[l], w_up[l], w_down[l])
        mod = _adaln(c_all, w_ada[l], b_ada[l])
        mods_p = [mod[:n, k * d:(k + 1) * d].reshape(n, 1, d) for k in range(6)]
        mods_s = [jnp.repeat(mod[n:, k * d:(k + 1) * d], ts, axis=0).reshape(ms // tms, tms, d) for k in range(6)]
        xp, sp = _prompt_layer(xp, mods_p, lw, slopes, n, t, final)
        caches = (cache_sb_kv[l], cache_cmp_kv[l], cache_sel_kv[l], state_win_kv[l])
        xs, ss = _sample_layer(xs, mods_s, lw, slopes, caches, page_table, ns, ts, final)
        kv_p = (n, t, 2, NSA_KV_HEADS, D_HEAD)
        kv_s = (ns, ts, 2, NSA_KV_HEADS, D_HEAD)
        win_new = jnp.concatenate([state_win_kv[l], ss[3].reshape(kv_s)], axis=1)
        sp = (sp[0].reshape(n, t, 2, SB_HEADS, D_HEAD), sp[1].reshape(kv_p), sp[2].reshape(kv_p),
              sp[3].reshape(kv_p)[:, t - min(WINDOW, t):])
        ss = (ss[0].reshape(ns, ts, 2, SB_HEADS, D_HEAD), ss[1].reshape(kv_s), ss[2].reshape(kv_s),
              win_new[:, win_new.shape[1] - min(WINDOW, win_new.shape[1]):])
        for acc, s in zip(st_p, sp):
            acc.append(s)
        for acc, s in zip(st_s, ss):
            acc.append(s)
    return (xp.reshape(n, t, d), xs.reshape(ns, ts, d), jnp.stack(st_p[0]), jnp.stack(st_s[0]),
            jnp.stack(st_p[1]), jnp.stack(st_s[1]), jnp.stack(st_p[2]), jnp.stack(st_s[2]),
            jnp.stack(st_p[3]), jnp.stack(st_s[3]))
```

```python
import functools
import math

import jax
import jax.numpy as jnp
from jax import lax
from jax.experimental import pallas as pl
from jax.experimental.pallas import tpu as pltpu

F32 = jnp.float32
BF16 = jnp.bfloat16

D_HEAD = 64
SB_HEADS = 8
NSA_HEADS = 8
NSA_KV_HEADS = 2
NSA_REP = NSA_HEADS // NSA_KV_HEADS
SB_WIDTH = SB_HEADS * D_HEAD
NSA_WIDTH = NSA_HEADS * D_HEAD
KV_WIDTH = 2 * NSA_KV_HEADS * D_HEAD
N_NSA_BRANCHES = 3
CMP_BLOCK = 32
SEL_BLOCK = 64
TOP_N = 16
WINDOW = 512
PAGE_SIZE = 128
RMS_EPS = 1e-6
FORCE_SCORE = 1e4
ATTN_SCALE = 0.125
LANES = 128
NEG = -1e30
SB_GROUP = 4
KEY_GROUP = 256
VMEM_LIMIT = 56 * 1024 * 1024

_C_SBQ = 0
_C_SBKV = _C_SBQ + SB_WIDTH
_C_NSQ = _C_SBKV + 2 * SB_WIDTH
_C_CMP = _C_NSQ + NSA_WIDTH
_C_SEL = _C_CMP + KV_WIDTH
_C_WIN = _C_SEL + KV_WIDTH
_C_GATE = _C_WIN + KV_WIDTH
_C_GSB = _C_GATE + N_NSA_BRANCHES * NSA_HEADS


def _params(*sem):
    return pltpu.CompilerParams(dimension_semantics=sem, vmem_limit_bytes=VMEM_LIMIT)


def _nt(a, b):
    return lax.dot_general(a, b, (((1,), (1,)), ((), ())), preferred_element_type=F32)


def _mm(a, b):
    return jnp.dot(a, b, preferred_element_type=F32)


def _modulated_norm(x, g, sc, sh):
    y = x * lax.rsqrt(jnp.mean(x * x, axis=-1, keepdims=True) + RMS_EPS)
    return (y * g) * (1.0 + sc) + sh


def _adaln_kernel(c_ref, w_ref, b_ref, o_ref):
    c = c_ref[...]
    s = c * jax.nn.sigmoid(c)
    o_ref[...] = _mm(s.astype(BF16), w_ref[...].astype(BF16)) + b_ref[...]


def _adaln(c, w_ada, b_ada):
    n, d = c.shape
    n_out = w_ada.shape[1]
    tn = 512
    return pl.pallas_call(
        _adaln_kernel,
        grid=(n_out // tn,),
        in_specs=[pl.BlockSpec((n, d), lambda j: (0, 0)),
                  pl.BlockSpec((d, tn), lambda j: (0, j)),
                  pl.BlockSpec((1, tn), lambda j: (0, j))],
        out_specs=pl.BlockSpec((n, tn), lambda j: (0, j)),
        out_shape=jax.ShapeDtypeStruct((n, n_out), F32),
        compiler_params=_params("arbitrary"),
        name="adaln",
    )(c, w_ada, b_ada.reshape(1, n_out))


def _proj_kernel(x_ref, g_ref, sc_ref, sh_ref, w_ref, *rest, groups, t_groups):
    wt_ref, out_refs = (rest[0], rest[1:]) if t_groups else (None, rest)
    hb = _modulated_norm(x_ref[...], g_ref[...], sc_ref[...], sh_ref[...]).astype(BF16)
    k = 0
    for start, width, act, dtypes in groups:
        z = _mm(hb, w_ref[:, start:start + width])
        if act == "sigmoid":
            z = jax.nn.sigmoid(z)
        for dt in dtypes:
            out_refs[k][...] = z.astype(dt)
            k += 1
    for start, width in t_groups:
        out_refs[k][...] = _nt(wt_ref[start:start + width, :], hb)
        k += 1


def _mod_spec(mod, tm, rows_per_seq):
    d = mod.shape[-1]
    if mod.shape[1] == 1:
        tiles = rows_per_seq // tm
        return pl.BlockSpec((None, 1, d), lambda i: (i // tiles, 0, 0))
    return pl.BlockSpec((None, tm, d), lambda i: (i, 0, 0))


def _project(x, g, sc, sh, w_cat, groups, tm, rows_per_seq, w_t=None, t_groups=()):
    m, d = x.shape
    out_shape, out_specs = [], []
    for _, width, _, dtypes in groups:
        for dt in dtypes:
            out_shape.append(jax.ShapeDtypeStruct((m, width), dt))
            out_specs.append(pl.BlockSpec((tm, width), lambda i: (i, 0)))
    tiles = rows_per_seq // tm
    for _, width in t_groups:
        out_shape.append(jax.ShapeDtypeStruct((m // rows_per_seq, width, rows_per_seq), F32))
        out_specs.append(pl.BlockSpec((None, width, tm), lambda i: (i // tiles, 0, lax.rem(i, tiles))))
    operands = [x, g, sc, sh, w_cat]
    in_specs = [pl.BlockSpec((tm, d), lambda i: (i, 0)),
                pl.BlockSpec((1, d), lambda i: (0, 0)),
                _mod_spec(sc, tm, rows_per_seq), _mod_spec(sh, tm, rows_per_seq),
                pl.BlockSpec(w_cat.shape, lambda i: (0, 0))]
    if t_groups:
        operands.append(w_t)
        in_specs.append(pl.BlockSpec(w_t.shape, lambda i: (0, 0)))
    return pl.pallas_call(
        functools.partial(_proj_kernel, groups=groups, t_groups=tuple(t_groups)),
        grid=(m // tm,),
        in_specs=in_specs,
        out_specs=out_specs,
        out_shape=out_shape,
        compiler_params=_params("arbitrary"),
        name="in_proj",
    )(*operands)


def _mix_kernel(x_ref, g_ref, sc_ref, sh_ref, gt_ref, osb_ref, onsa_ref, wg_ref, wbs_ref, wbn_ref, wo_ref, o_ref):
    x = x_ref[...]
    d = x.shape[1]
    hb = _modulated_norm(x, g_ref[...], sc_ref[...], sh_ref[...]).astype(BF16)
    gate_sb = jax.nn.sigmoid(_mm(hb, wg_ref[:, :d]))
    gate_nsa = jax.nn.sigmoid(_mm(hb, wg_ref[:, d:]))
    mixed = gate_sb * _mm(osb_ref[...], wbs_ref[...]) + gate_nsa * _mm(onsa_ref[...], wbn_ref[...])
    o_ref[...] = x + gt_ref[...] * _mm(mixed.astype(BF16), wo_ref[...])


def _mix(x, g, sc, sh, gt, o_sb, o_nsa, w_gate, w_br_sb, w_br_nsa, w_out, tm, rows_per_seq):
    m, d = x.shape
    full = lambda a: pl.BlockSpec(a.shape, lambda i: (0, 0))
    return pl.pallas_call(
        _mix_kernel,
        grid=(m // tm,),
        in_specs=[pl.BlockSpec((tm, d), lambda i: (i, 0)), pl.BlockSpec((1, d), lambda i: (0, 0)),
                  _mod_spec(sc, tm, rows_per_seq), _mod_spec(sh, tm, rows_per_seq), _mod_spec(gt, tm, rows_per_seq),
                  pl.BlockSpec((tm, o_sb.shape[1]), lambda i: (i, 0)),
                  pl.BlockSpec((tm, o_nsa.shape[1]), lambda i: (i, 0)),
                  full(w_gate), full(w_br_sb), full(w_br_nsa), full(w_out)],
        out_specs=pl.BlockSpec((tm, d), lambda i: (i, 0)),
        out_shape=jax.ShapeDtypeStruct((m, d), F32),
        compiler_params=_params("arbitrary"),
        name="mix_out",
    )(x, g, sc, sh, gt, o_sb, o_nsa, w_gate, w_br_sb, w_br_nsa, w_out)


def _mlp_kernel(x_ref, g_ref, sc_ref, sh_ref, gt_ref, gf_ref, wu_ref, wd_ref, o_ref, *, final):
    x = x_ref[...]
    hb = _modulated_norm(x, g_ref[...], sc_ref[...], sh_ref[...]).astype(BF16)
    u = jnp.square(jnp.maximum(_mm(hb, wu_ref[...]), 0.0))
    x2 = x + gt_ref[...] * _mm(u.astype(BF16), wd_ref[...])
    if final:
        x2 = x2 * lax.rsqrt(jnp.mean(x2 * x2, axis=-1, keepdims=True) + RMS_EPS) * gf_ref[...]
    o_ref[...] = x2


def _mlp(x, g, sc, sh, gt, g_final, w_up, w_down, tm, rows_per_seq, final):
    m, d = x.shape
    full = lambda a: pl.BlockSpec(a.shape, lambda i: (0, 0))
    return pl.pallas_call(
        functools.partial(_mlp_kernel, final=final),
        grid=(m // tm,),
        in_specs=[pl.BlockSpec((tm, d), lambda i: (i, 0)), pl.BlockSpec((1, d), lambda i: (0, 0)),
                  _mod_spec(sc, tm, rows_per_seq), _mod_spec(sh, tm, rows_per_seq), _mod_spec(gt, tm, rows_per_seq),
                  pl.BlockSpec((1, d), lambda i: (0, 0)), full(w_up), full(w_down)],
        out_specs=pl.BlockSpec((tm, d), lambda i: (i, 0)),
        out_shape=jax.ShapeDtypeStruct((m, d), F32),
        compiler_params=_params("arbitrary"),
        name="mlp_final",
    )(x, g, sc, sh, gt, g_final, w_up, w_down)


def _log_sigmoids(z):
    sp = jnp.log(1.0 + jnp.exp(-jnp.abs(z)))
    return jnp.minimum(z, 0.0) - sp, -jnp.maximum(z, 0.0) - sp


def _split_bf16(x):
    hi = x.astype(BF16)
    return hi, (x - hi.astype(F32)).astype(BF16)


def _split3_bf16(x):
    hi = x.astype(BF16)
    r1 = x - hi.astype(F32)
    mid = r1.astype(BF16)
    return hi, mid, (r1 - mid.astype(F32)).astype(BF16)


def _later_and_rowsum():
    r = lax.broadcasted_iota(jnp.int32, (2 * LANES, 2 * LANES), 0)
    c = lax.broadcasted_iota(jnp.int32, (2 * LANES, 2 * LANES), 1)
    return jnp.where((c >= LANES) | ((r & (LANES - 1)) > c), 1.0, 0.0).astype(BF16)


def _sb_group(z, msk, pv, later, acc_ref, carry_ref, latest_first):
    rows = z.shape[0]
    nb = z.shape[1] // LANES
    lsz, l1m = _log_sigmoids(z)
    if msk is not None:
        l1m = jnp.where(msk, l1m, 0.0)
    hi, lo = _split_bf16(l1m)
    stack = lambda x: jnp.concatenate([x[:, u * LANES:(u + 1) * LANES] for u in range(nb)], axis=0)
    cs = _mm(jnp.concatenate([stack(hi), stack(lo)], axis=1), later)
    carry = carry_ref[...]
    parts = [None] * nb
    for u in (range(nb) if latest_first else reversed(range(nb))):
        c_u = cs[u * rows:(u + 1) * rows]
        parts[u] = lsz[:, u * LANES:(u + 1) * LANES] + c_u[:, :LANES] + carry
        carry = carry + c_u[:, LANES:]
    carry_ref[...] = carry
    a = jnp.exp(jnp.concatenate(parts, axis=1))
    if msk is not None:
        a = jnp.where(msk, a, 0.0)
    acc_ref[...] += pv(a.astype(BF16))


def _sb_prompt_kernel(q_ref, k_ref, v_ref, o_ref, acc_ref, carry_ref, *, tq):
    i = pl.program_id(2)
    nb = tq // LANES
    lane = lax.broadcasted_iota(jnp.int32, (1, LANES), 1)
    q = q_ref[...]
    zero = jnp.zeros_like(q)
    qs = jnp.concatenate([jnp.where(lane < D_HEAD, q, zero), jnp.where(lane >= D_HEAD, q, zero)], axis=0)
    later = _later_and_rowsum()
    q_pos = i * tq + lax.rem(lax.broadcasted_iota(jnp.int32, (2 * tq, 1), 0), tq)
    acc_ref[...] = jnp.zeros_like(acc_ref)
    carry_ref[...] = jnp.zeros_like(carry_ref)

    def group(j0, n_tiles, masked):
        off = pl.multiple_of(j0 * LANES, LANES)
        kb = k_ref[pl.ds(off, n_tiles * LANES), :]
        vb = v_ref[pl.ds(off, n_tiles * LANES), :]
        msk = None
        if masked:
            msk = (off + lax.broadcasted_iota(jnp.int32, (1, n_tiles * LANES), 1)) < q_pos
        _sb_group(_nt(qs, kb), msk, lambda a: _mm(a, vb), later, acc_ref, carry_ref, latest_first=False)

    group(i * nb, nb, True)
    below = i * nb
    rem = lax.rem(below, SB_GROUP)
    for r in range(1, SB_GROUP):
        @pl.when(rem == r)
        def _(r=r):
            group(below - r, r, False)

    def body(jj, c):
        group(below - rem - SB_GROUP * (jj + 1), SB_GROUP, False)
        return c

    lax.fori_loop(0, below // SB_GROUP, body, 0)
    acc = acc_ref[...]
    o_ref[...] = jnp.where(lane < D_HEAD, acc[:tq], acc[tq:]).astype(o_ref.dtype)


def _sb_prompt(q, kv, n, t, tq):
    pairs = SB_WIDTH // LANES
    nq = t // tq
    return pl.pallas_call(
        functools.partial(_sb_prompt_kernel, tq=tq),
        grid=(n, pairs, nq),
        in_specs=[pl.BlockSpec((tq, LANES), lambda b, p, i: (b * nq + i, p)),
                  pl.BlockSpec((t, LANES), lambda b, p, i: (b, p)),
                  pl.BlockSpec((t, LANES), lambda b, p, i: (b, pairs + p))],
        out_specs=pl.BlockSpec((tq, LANES), lambda b, p, i: (b * nq + i, p)),
        out_shape=jax.ShapeDtypeStruct((n * t, SB_WIDTH), BF16),
        scratch_shapes=[pltpu.VMEM((2 * tq, LANES), F32), pltpu.VMEM((2 * tq, LANES), F32)],
        compiler_params=_params("arbitrary", "arbitrary", "arbitrary"),
        name="sb_prompt",
    )(q, kv, kv)


def _nsa_prompt_kernel(slopes_ref, q_ref, cmp_ref, cw_ref, ksel_ref, vsel_ref, kwin_ref, vwin_ref,
                       g0_ref, g1_ref, g2_ref, o_ref, ck_ref, s_ref, mrun_ref, acc_ref, osel_ref, *, tq, t_len):
    g = pl.program_id(1)
    i = pl.program_id(2)
    nsel = t_len // SEL_BLOCK
    rows = NSA_REP * tq
    lane = lax.broadcasted_iota(jnp.int32, (1, LANES), 1)
    low = lane < D_HEAD

    @pl.when(i == 0)
    def _compress():
        xr = cmp_ref[...].reshape(nsel, SEL_BLOCK, KV_WIDTH)
        cw = cw_ref[...][None]
        own = jnp.where(low, 0, 1) == g
        halves = (jnp.sum(xr[:, :CMP_BLOCK, :] * cw, axis=1), jnp.sum(xr[:, CMP_BLOCK:, :] * cw, axis=1))
        for e, c in enumerate(halves):
            for kv in range(2):
                part = c[:, kv * LANES:(kv + 1) * LANES]
                ck_ref[2 * kv + e] = jnp.where(own, part, pltpu.roll(part, D_HEAD, 1)).astype(BF16)

    q = q_ref[...]
    zero = jnp.zeros((tq, LANES), q.dtype)
    pieces = []
    for b in range(NSA_REP // 2):
        qb = q[:, b * LANES:(b + 1) * LANES]
        pieces += [jnp.where(low, qb, zero), jnp.where(low, zero, qb)]
    qs = jnp.concatenate(pieces, axis=0)
    slope = [slopes_ref[g * NSA_REP + r] for r in range(NSA_REP)]
    slope_col = jnp.concatenate([jnp.full((tq, 1), slope[r], F32) for r in range(NSA_REP)], axis=0)
    t_col = i * tq + lax.rem(lax.broadcasted_iota(jnp.int32, (rows, 1), 0), tq)

    kde, kdo, vde, vdo = ck_ref[0], ck_ref[1], ck_ref[2], ck_ref[3]

    jl = lax.broadcasted_iota(jnp.int32, (1, nsel), 1)
    end_e = jl * SEL_BLOCK + (CMP_BLOCK - 1)
    end_o = jl * SEL_BLOCK + (SEL_BLOCK - 1)
    s_e = _nt(qs, kde) - slope_col * (t_col - end_e).astype(F32)
    s_o = _nt(qs, kdo) - slope_col * (t_col - end_o).astype(F32)
    ok_e = end_e <= t_col
    ok_o = end_o <= t_col
    mx = jnp.maximum(jnp.max(jnp.where(ok_e, s_e, NEG), axis=1, keepdims=True),
                     jnp.max(jnp.where(ok_o, s_o, NEG), axis=1, keepdims=True))
    mx = jnp.where(mx > 0.5 * NEG, mx, 0.0)
    e_e = jnp.where(ok_e, jnp.exp(s_e - mx), 0.0)
    e_o = jnp.where(ok_o, jnp.exp(s_o - mx), 0.0)
    den = jnp.maximum(jnp.sum(e_e, axis=1, keepdims=True) + jnp.sum(e_o, axis=1, keepdims=True), 1e-30)
    o_cmp = _mm((e_e / den).astype(BF16), vde) + _mm((e_o / den).astype(BF16), vdo)

    jr = lax.broadcasted_iota(jnp.int32, (nsel, tq), 0)
    tl = i * tq + lax.broadcasted_iota(jnp.int32, (nsel, tq), 1)
    end_et = jr * SEL_BLOCK + (CMP_BLOCK - 1)
    end_ot = jr * SEL_BLOCK + (SEL_BLOCK - 1)
    ok_et = end_et <= tl
    ok_ot = end_ot <= tl
    st_e = _nt(kde, qs)
    st_o = _nt(kdo, qs)
    score = jnp.zeros((nsel, tq), F32)
    for r in range(NSA_REP):
        a_e = st_e[:, r * tq:(r + 1) * tq] - slope[r] * (tl - end_et).astype(F32)
        a_o = st_o[:, r * tq:(r + 1) * tq] - slope[r] * (tl - end_ot).astype(F32)
        mt = jnp.maximum(jnp.max(jnp.where(ok_et, a_e, NEG), axis=0, keepdims=True),
                         jnp.max(jnp.where(ok_ot, a_o, NEG), axis=0, keepdims=True))
        mt = jnp.where(mt > 0.5 * NEG, mt, 0.0)
        x_e = jnp.where(ok_et, jnp.exp(a_e - mt), 0.0)
        x_o = jnp.where(ok_ot, jnp.exp(a_o - mt), 0.0)
        dt = jnp.maximum(jnp.sum(x_e, axis=0, keepdims=True) + jnp.sum(x_o, axis=0, keepdims=True), 1e-30)
        score = score + (x_e / dt + x_o / dt)
    cur = tl // SEL_BLOCK
    forced = (jr == 0) | (jr == cur) | (jr == cur - 1)
    score = jnp.where(forced, FORCE_SCORE, jnp.where(jr * SEL_BLOCK <= tl, score, -1.0))
    rank = jnp.zeros((nsel, tq), jnp.int32)
    for j2 in range(nsel):
        other = score[j2:j2 + 1, :]
        ahead = (other > score) | ((other == score) & (jr > j2))
        rank = rank + jnp.where(ahead, 1, 0)
    sel_t = jnp.where(rank < min(TOP_N, nsel), 1.0, 0.0).astype(BF16)
    eye_r = lax.broadcasted_iota(jnp.int32, (tq, tq), 0)
    eye_c = lax.broadcasted_iota(jnp.int32, (tq, tq), 1)
    eye = jnp.where(eye_r == eye_c, 1.0, 0.0).astype(BF16)
    sel = _nt(eye, sel_t).astype(BF16)

    t0 = i * tq
    lane_g = lax.broadcasted_iota(jnp.int32, (1, KEY_GROUP), 1)
    row_minus_lane = (lax.broadcasted_iota(jnp.int32, (tq, KEY_GROUP), 0)
                      - lax.broadcasted_iota(jnp.int32, (tq, KEY_GROUP), 1))
    ones_v = jnp.ones((KEY_GROUP, LANES), BF16)
    g_hi = (t0 + tq + KEY_GROUP - 1) // KEY_GROUP

    def for_groups(g_lo, body):
        n = g_hi - g_lo

        def pair(jj, c):
            body(g_lo + 2 * jj)
            body(g_lo + 2 * jj + 1)
            return c

        lax.fori_loop(0, n // 2, pair, 0)

        @pl.when(lax.rem(n, 2) == 1)
        def _():
            body(g_hi - 1)

    def sweep(k_ref_, v_ref_, g_lo, mask_bias):
        mrun_ref[...] = jnp.full_like(mrun_ref, NEG)

        def pass1(gi):
            off = pl.multiple_of(gi * KEY_GROUP, KEY_GROUP)
            add = mask_bias(gi, row_minus_lane + (t0 - off))
            k_rel = (off - t0 + lane_g).astype(F32)
            raw = _nt(qs, k_ref_[pl.ds(off, KEY_GROUP), :])
            s = jnp.concatenate([raw[r * tq:(r + 1) * tq] + (add + slope[r] * k_rel) for r in range(NSA_REP)],
                                axis=0)
            s_ref[gi - g_lo] = s
            mrun_ref[...] = jnp.maximum(mrun_ref[...], s)

        for_groups(g_lo, pass1)
        mrun_ref[...] = jnp.broadcast_to(jnp.max(mrun_ref[...], axis=1, keepdims=True), mrun_ref.shape)
        acc_ref[...] = jnp.zeros_like(acc_ref)

        def pass2(gi):
            off = pl.multiple_of(gi * KEY_GROUP, KEY_GROUP)
            p = jnp.exp(s_ref[gi - g_lo] - mrun_ref[...]).astype(BF16)
            acc_ref[...] += _mm(p, jnp.concatenate([v_ref_[pl.ds(off, KEY_GROUP), :], ones_v], axis=1))

        for_groups(g_lo, pass2)
        acc = acc_ref[...]
        return acc[:, :LANES] / jnp.maximum(acc[:, LANES:], 1e-30)

    per_group = KEY_GROUP // SEL_BLOCK
    br = lax.broadcasted_iota(jnp.int32, (nsel, KEY_GROUP), 0)
    bc = lax.broadcasted_iota(jnp.int32, (nsel, KEY_GROUP), 1) // SEL_BLOCK
    not_sel = ((sel.astype(F32) - 1.0) * (-NEG)).astype(BF16)

    def sel_bias(gi, dist):
        expand = jnp.where(br == gi * per_group + bc, 1.0, 0.0).astype(BF16)
        return jnp.where(dist >= 0, _mm(not_sel, expand), NEG)

    osel_ref[...] = sweep(ksel_ref, vsel_ref, 0, sel_bias)

    def win_bias(gi, dist):
        return jnp.where((dist >= 0) & (dist < WINDOW), 0.0, NEG)

    o_win = sweep(kwin_ref, vwin_ref, jnp.maximum(t0 - (WINDOW - 1), 0) // KEY_GROUP, win_bias)
    o_sel = osel_ref[...]

    for b in range(NSA_REP // 2):
        lo_rows = slice(2 * b * tq, (2 * b + 1) * tq)
        hi_rows = slice((2 * b + 1) * tq, (2 * b + 2) * tq)
        cols = slice(b * LANES, (b + 1) * LANES)
        pick = lambda o: jnp.where(low, o[lo_rows], o[hi_rows])
        merged = (g0_ref[:, cols].astype(F32) * pick(o_cmp) + g1_ref[:, cols].astype(F32) * pick(o_sel)
                  + g2_ref[:, cols].astype(F32) * pick(o_win))
        o_ref[:, cols] = merged.astype(o_ref.dtype)


def _nsa_prompt(slopes, q, cmp_kv, cw, sel_dup, win_dup, gates, n, t, tq):
    nq = t // tq
    gw = NSA_REP * D_HEAD
    row = lambda b, g, i: b * nq + i
    seq = lambda c: pl.BlockSpec((t, LANES), lambda b, g, i, c=c: (b, c * NSA_KV_HEADS + g))
    gate = lambda br: pl.BlockSpec((tq, gw), lambda b, g, i, br=br: (row(b, g, i), br * NSA_KV_HEADS + g))
    nsel = t // SEL_BLOCK
    rows = NSA_REP * tq
    return pl.pallas_call(
        functools.partial(_nsa_prompt_kernel, tq=tq, t_len=t),
        grid=(n, NSA_KV_HEADS, nq),
        in_specs=[pl.BlockSpec(memory_space=pltpu.SMEM),
                  pl.BlockSpec((tq, gw), lambda b, g, i: (row(b, g, i), g)),
                  pl.BlockSpec((t, KV_WIDTH), lambda b, g, i: (b, 0)),
                  pl.BlockSpec((CMP_BLOCK, KV_WIDTH), lambda b, g, i: (0, 0)),
                  seq(0), seq(1), seq(0), seq(1), gate(0), gate(1), gate(2)],
        out_specs=pl.BlockSpec((tq, gw), lambda b, g, i: (row(b, g, i), g)),
        out_shape=jax.ShapeDtypeStruct((n * t, NSA_WIDTH), BF16),
        scratch_shapes=[pltpu.VMEM((4, nsel, LANES), BF16), pltpu.VMEM((t // KEY_GROUP, rows, KEY_GROUP), F32),
                        pltpu.VMEM((rows, KEY_GROUP), F32), pltpu.VMEM((rows, 2 * LANES), F32),
                        pltpu.VMEM((rows, LANES), F32)],
        compiler_params=_params("arbitrary", "arbitrary", "arbitrary"),
        name="nsa_prompt",
    )(slopes, q, cmp_kv, cw, sel_dup, sel_dup, win_dup, win_dup, gates, gates, gates)


def _sb_sample_kernel(pt_ref, q_ref, new_ref, *rest, n_step, ts):
    page_refs, o_ref, acc_ref, carry_ref = rest[:n_step], rest[n_step], rest[n_step + 1], rest[n_step + 2]
    jj = pl.program_id(1)
    rows = ts * SB_HEADS
    rr = lax.broadcasted_iota(jnp.int32, (rows, SB_WIDTH), 0)
    cc = lax.broadcasted_iota(jnp.int32, (rows, SB_WIDTH), 1)
    own_head = (cc // D_HEAD) == lax.rem(rr, SB_HEADS)
    q = q_ref[...]
    qrows = jnp.where(own_head, q, jnp.zeros_like(q))
    later = _later_and_rowsum()

    @pl.when(jj == 0)
    def _first():
        acc_ref[...] = jnp.zeros_like(acc_ref)
        carry_ref[...] = jnp.zeros_like(carry_ref)
        new = new_ref[...]
        s_key = lax.broadcasted_iota(jnp.int32, (rows, PAGE_SIZE), 1)
        t_qry = lax.broadcasted_iota(jnp.int32, (rows, PAGE_SIZE), 0) // SB_HEADS
        _sb_group(_nt(qrows, new[:, :SB_WIDTH]), s_key < t_qry, lambda a: _mm(a, new[:, SB_WIDTH:]),
                  later, acc_ref, carry_ref, latest_first=True)

    k_t = jnp.concatenate([p[:SB_WIDTH, :] for p in page_refs], axis=1).astype(BF16)
    v_t = jnp.concatenate([p[SB_WIDTH:, :] for p in page_refs], axis=1).astype(BF16)
    _sb_group(_mm(qrows, k_t), None, lambda a: _nt(a, v_t), later, acc_ref, carry_ref, latest_first=True)

    @pl.when(jj == pl.num_programs(1) - 1)
    def _last():
        own = jnp.where(own_head, acc_ref[...], 0.0)
        o_ref[...] = jnp.sum(own.reshape(ts, SB_HEADS, SB_WIDTH), axis=1).astype(o_ref.dtype)


def _sb_sample(page_table, q_rep, kv_new, cache, n_step):
    ns, rows, _ = q_rep.shape
    ts = rows // SB_HEADS
    n_pages = page_table.shape[1]
    steps = n_pages // n_step

    def page_spec(u):
        def index(b, jj, pt):
            return (pt[b * n_pages + (n_pages - 1 - (jj * n_step + u))], 0, 0)
        return pl.BlockSpec((None, 2 * SB_WIDTH, PAGE_SIZE), index)

    grid_spec = pltpu.PrefetchScalarGridSpec(
        num_scalar_prefetch=1,
        grid=(ns, steps),
        in_specs=[pl.BlockSpec((None, rows, SB_WIDTH), lambda b, jj, pt: (b, 0, 0)),
                  pl.BlockSpec((None, PAGE_SIZE, 2 * SB_WIDTH), lambda b, jj, pt: (b, 0, 0))]
                 + [page_spec(u) for u in range(n_step)],
        out_specs=pl.BlockSpec((None, ts, SB_WIDTH), lambda b, jj, pt: (b, 0, 0)),
        scratch_shapes=[pltpu.VMEM((rows, SB_WIDTH), F32), pltpu.VMEM((rows, LANES), F32)])
    return pl.pallas_call(
        functools.partial(_sb_sample_kernel, n_step=n_step, ts=ts),
        grid_spec=grid_spec,
        out_shape=jax.ShapeDtypeStruct((ns, ts, SB_WIDTH), BF16),
        compiler_params=_params("arbitrary", "arbitrary"),
        name="sb_sample",
    )(page_table.reshape(-1), q_rep, kv_new, *([cache] * n_step))


def _pool_matrices(n_pages, n_step):
    steps = n_pages // n_step
    nps = n_pages * PAGE_SIZE // SEL_BLOCK
    key = jnp.arange(steps * n_step * PAGE_SIZE)
    col = (key // CMP_BLOCK % 2) * nps + key // SEL_BLOCK
    return jax.nn.one_hot(col, 2 * nps, dtype=BF16).reshape(steps, n_step * PAGE_SIZE, 2 * nps)


def _cmp_sample_kernel(pt_ref, slopes_ref, q_ref, cw_ref, pool_ref, *rest, n_step, ts, past):
    page_refs = rest[:n_step]
    o_ref, idx_ref, ck_ref = rest[n_step:]
    jj = pl.program_id(1)
    nps = past // SEL_BLOCK
    rows = q_ref.shape[1]
    n_cols = NSA_KV_HEADS * ts

    @pl.when(jj == 0)
    def _zero():
        ck_ref[...] = jnp.zeros_like(ck_ref)

    cw = cw_ref[...]
    weighted = jnp.concatenate([r[...] * cw for r in page_refs], axis=1)
    pooled = _mm(jnp.concatenate(_split_bf16(weighted), axis=0), pool_ref[jj])
    ck_ref[...] += pooled[:KV_WIDTH] + pooled[KV_WIDTH:]

    @pl.when(jj == pl.num_programs(1) - 1)
    def _attend():
        ck = ck_ref[...]
        ke_t, ko_t = ck[:LANES, :nps].astype(BF16), ck[:LANES, nps:].astype(BF16)
        ve_t, vo_t = ck[LANES:, :nps].astype(BF16), ck[LANES:, nps:].astype(BF16)
        jl = lax.broadcasted_iota(jnp.int32, (rows, nps), 1)
        rl = lax.broadcasted_iota(jnp.int32, (rows, nps), 0)
        pos_n = past + lax.rem(rl, ts)
        d_e = pos_n - (jl * SEL_BLOCK + CMP_BLOCK - 1)
        d_o = pos_n - (jl * SEL_BLOCK + SEL_BLOCK - 1)
        p_sum = jnp.zeros((rows, nps), F32)
        for r in range(NSA_REP):
            q = q_ref[r]
            sl = [slopes_ref[g * NSA_REP + r] for g in range(NSA_KV_HEADS)]
            slope = jnp.where(rl // ts == 0, sl[0], sl[1])
            s_e = _mm(q, ke_t) - slope * d_e.astype(F32)
            s_o = _mm(q, ko_t) - slope * d_o.astype(F32)
            mx = jnp.maximum(jnp.max(jnp.where(d_e >= 0, s_e, NEG), axis=1, keepdims=True),
                             jnp.max(jnp.where(d_o >= 0, s_o, NEG), axis=1, keepdims=True))
            mx = jnp.where(mx > 0.5 * NEG, mx, 0.0)
            e_e = jnp.where(d_e >= 0, jnp.exp(s_e - mx), 0.0)
            e_o = jnp.where(d_o >= 0, jnp.exp(s_o - mx), 0.0)
            den = jnp.maximum(jnp.sum(e_e, axis=1, keepdims=True) + jnp.sum(e_o, axis=1, keepdims=True), 1e-30)
            p_e, p_o = e_e / den, e_o / den
            o_ref[r] = _nt(p_e.astype(BF16), ve_t) + _nt(p_o.astype(BF16), vo_t)
            p_sum = p_sum + (p_e + p_o)
        cur = pos_n // SEL_BLOCK
        forced = (jl == 0) | (jl == cur) | (jl == cur - 1)
        score = jnp.where(forced, FORCE_SCORE, jnp.where(jl * SEL_BLOCK <= pos_n, p_sum, -1.0))
        pos_1 = past + lax.rem(lax.broadcasted_iota(jnp.int32, (rows, 1), 0), ts)
        cur_1 = pos_1 // SEL_BLOCK
        forced_1 = (cur_1 == nps) | (cur_1 - 1 == nps)
        tail = jnp.where(forced_1, FORCE_SCORE, jnp.where(nps * SEL_BLOCK <= pos_1, 0.0, -1.0))
        er = lax.broadcasted_iota(jnp.int32, (nps, nps), 0)
        ec = lax.broadcasted_iota(jnp.int32, (nps, nps), 1)
        eye = jnp.where(er == ec, 1.0, 0.0).astype(BF16)
        score_t = sum(_nt(eye, piece) for piece in _split3_bf16(score))
        k_f = lax.broadcasted_iota(jnp.int32, (TOP_N, nps), 0).astype(F32)
        j_f = lax.broadcasted_iota(jnp.int32, (TOP_N, nps), 1).astype(F32)
        k_col = lax.broadcasted_iota(jnp.int32, (TOP_N, 1), 0).astype(F32)
        c_lane = lax.broadcasted_iota(jnp.int32, (TOP_N, rows), 1)
        picked = jnp.zeros((TOP_N, rows), F32)
        for c in range(n_cols):
            other = score_t[:, c:c + 1]
            mine = score[c:c + 1, :]
            tail_c = tail[c:c + 1, :]
            ahead = (other > mine) | ((other == mine) & (er < ec))
            rank = jnp.sum(jnp.where(ahead, 1.0, 0.0), axis=0, keepdims=True) + jnp.where(tail_c > mine, 1.0, 0.0)
            rank_tail = jnp.sum(jnp.where(mine >= tail_c, 1.0, 0.0), axis=1, keepdims=True)
            hit = jnp.sum(jnp.where(rank == k_f, j_f, 0.0), axis=1, keepdims=True)
            hit = hit + jnp.where(rank_tail == k_col, float(nps), 0.0)
            picked = jnp.where(c_lane == c, hit, picked)
        idx_ref[...] = picked.astype(jnp.int32)


def _cmp_sample(page_table, slopes, q, cw_t, cache, n_step, ts):
    ns, _, rows, _ = q.shape
    n_pages = page_table.shape[1]
    steps = n_pages // n_step
    past = n_pages * PAGE_SIZE
    nps = past // SEL_BLOCK
    pool = _pool_matrices(n_pages, n_step)

    def page_spec(u):
        return pl.BlockSpec((None, KV_WIDTH, PAGE_SIZE),
                            lambda b, jj, pt: (pt[b * n_pages + jj * n_step + u], 0, 0))

    grid_spec = pltpu.PrefetchScalarGridSpec(
        num_scalar_prefetch=1,
        grid=(ns, steps),
        in_specs=[pl.BlockSpec(memory_space=pltpu.SMEM),
                  pl.BlockSpec((None, NSA_REP, rows, LANES), lambda b, jj, pt: (b, 0, 0, 0)),
                  pl.BlockSpec(cw_t.shape, lambda b, jj, pt: (0, 0)),
                  pl.BlockSpec(pool.shape, lambda b, jj, pt: (0, 0, 0))]
                 + [page_spec(u) for u in range(n_step)],
        out_specs=[pl.BlockSpec((None, NSA_REP, rows, LANES), lambda b, jj, pt: (b, 0, 0, 0)),
                   pl.BlockSpec((None, TOP_N, rows), lambda b, jj, pt: (b, 0, 0))],
        scratch_shapes=[pltpu.VMEM((KV_WIDTH, 2 * nps), F32)])
    return pl.pallas_call(
        functools.partial(_cmp_sample_kernel, n_step=n_step, ts=ts, past=past),
        grid_spec=grid_spec,
        out_shape=[jax.ShapeDtypeStruct((ns, NSA_REP, rows, LANES), F32),
                   jax.ShapeDtypeStruct((ns, TOP_N, rows), jnp.int32)],
        compiler_params=_params("arbitrary", "arbitrary"),
        name="cmp_sample",
    )(page_table.reshape(-1), slopes, q, cw_t, pool, *([cache] * n_step))


def _sel_sample_kernel(page_ref, idx_ref, slopes_ref, q_ref, new_ref, *rest, ts, past, cols):
    all_refs, o_ref = rest[:ts * TOP_N], rest[ts * TOP_N]
    b, g = pl.program_id(0), pl.program_id(1)
    rows = q_ref.shape[1]
    rr = lax.broadcasted_iota(jnp.int32, (rows, 1), 0)
    slope = jnp.zeros((rows, 1), F32)
    for r in range(NSA_REP):
        slope = jnp.where(rr == r, slopes_ref[g * NSA_REP + r], slope)
    for t in range(ts):
        o_ref[t] = _sel_one_query(idx_ref, q_ref[t], new_ref, all_refs[t * TOP_N:(t + 1) * TOP_N], slope,
                                  (b * TOP_N) * cols + g * ts + t, cols, past, t)


def _sel_one_query(idx_ref, q, new_ref, kv_refs, slope, idx_base, cols, past, t):
    nps = past // SEL_BLOCK
    per_page = PAGE_SIZE // SEL_BLOCK
    width = TOP_N * PAGE_SIZE
    pos = past + t
    its = [idx_ref[idx_base + k * cols] for k in range(TOP_N)]
    ll = lax.broadcasted_iota(jnp.int32, (1, width), 1)
    block_of = jnp.zeros((1, width), jnp.int32)
    for k in range(TOP_N):
        block_of = jnp.where(ll // PAGE_SIZE == k, its[k], block_of)
    in_page = lax.rem(ll, PAGE_SIZE)
    k_pos = block_of * SEL_BLOCK + lax.rem(in_page, SEL_BLOCK)
    ok = (block_of < nps) & (in_page // SEL_BLOCK == lax.rem(block_of, per_page)) & (k_pos <= pos)
    k_t = jnp.concatenate([r[0] for r in kv_refs], axis=1).astype(BF16)
    v_t = jnp.concatenate([r[1] for r in kv_refs], axis=1).astype(BF16)
    s = _mm(q, k_t) - slope * (pos - k_pos).astype(F32)
    n_sel_new = jnp.where(its[0] >= nps, 1, 0)
    for k in range(1, TOP_N):
        n_sel_new = n_sel_new + jnp.where(its[k] >= nps, 1, 0)
    limit = jnp.where(n_sel_new > 0, pos, past - 1)
    n_new = new_ref.shape[1]
    kn_pos = past + lax.broadcasted_iota(jnp.int32, (1, n_new), 1)
    ok_n = kn_pos <= limit
    k_n = new_ref[0].astype(BF16)
    v_n = new_ref[1].astype(BF16)
    s_n = _nt(q, k_n) - slope * (pos - kn_pos).astype(F32)
    mx = jnp.maximum(jnp.max(jnp.where(ok, s, NEG), axis=1, keepdims=True),
                     jnp.max(jnp.where(ok_n, s_n, NEG), axis=1, keepdims=True))
    mx = jnp.where(mx > 0.5 * NEG, mx, 0.0)
    e = jnp.where(ok, jnp.exp(s - mx), 0.0)
    e_n = jnp.where(ok_n, jnp.exp(s_n - mx), 0.0)
    den = jnp.maximum(jnp.sum(e, axis=1, keepdims=True) + jnp.sum(e_n, axis=1, keepdims=True), 1e-30)
    return _nt((e / den).astype(BF16), v_t) + _mm((e_n / den).astype(BF16), v_n)


def _sel_sample(page_table, idx, slopes, q, kv_new, cache, ts):
    ns = q.shape[0]
    n_pages = page_table.shape[1]
    past = n_pages * PAGE_SIZE
    nps = past // SEL_BLOCK
    per_page = PAGE_SIZE // SEL_BLOCK
    cols = idx.shape[2]
    rows = q.shape[3]
    logical = jnp.minimum(idx, nps - 1) // per_page
    hot = (logical[..., None] == jnp.arange(n_pages)).astype(jnp.int32)
    pages = jnp.sum(hot * page_table[:, None, None, :], axis=-1)

    def slab_spec(t, k):
        def index(b, g, pg, ix):
            return (pg[(b * TOP_N + k) * cols + g * ts + t], 0, g, 0, 0)
        return pl.BlockSpec((None, 2, None, D_HEAD, PAGE_SIZE), index)

    grid_spec = pltpu.PrefetchScalarGridSpec(
        num_scalar_prefetch=2,
        grid=(ns, NSA_KV_HEADS),
        in_specs=[pl.BlockSpec(memory_space=pltpu.SMEM),
                  pl.BlockSpec((None, None, ts, rows, D_HEAD), lambda b, g, pg, ix: (b, g, 0, 0, 0)),
                  pl.BlockSpec((None, None) + kv_new.shape[2:], lambda b, g, pg, ix: (b, g, 0, 0, 0))]
                 + [slab_spec(t, k) for t in range(ts) for k in range(TOP_N)],
        out_specs=pl.BlockSpec((None, None, ts, rows, D_HEAD), lambda b, g, pg, ix: (b, g, 0, 0, 0)))
    return pl.pallas_call(
        functools.partial(_sel_sample_kernel, ts=ts, past=past, cols=cols),
        grid_spec=grid_spec,
        out_shape=jax.ShapeDtypeStruct((ns, NSA_KV_HEADS, ts, rows, D_HEAD), F32),
        compiler_params=_params("arbitrary", "arbitrary"),
        name="sel_sample",
    )(pages.reshape(-1), idx.reshape(-1), slopes, q, kv_new, *([cache] * (ts * TOP_N)))


def _win_sample_kernel(slopes_ref, q_ref, buf_ref, new_ref, o_ref, *, ts, past):
    rows = NSA_HEADS * ts
    w = buf_ref.shape[1]
    rr = lax.broadcasted_iota(jnp.int32, (rows, 1), 0)
    slope = jnp.zeros((rows, 1), F32)
    for h in range(NSA_HEADS):
        slope = jnp.where(rr // ts == h, slopes_ref[h], slope)
    pos = past + lax.rem(rr, ts)
    q = q_ref[...]
    new = new_ref[...]
    n_new = new.shape[0]
    kb_pos = past - w + lax.broadcasted_iota(jnp.int32, (1, w), 1)
    kn_pos = past + lax.broadcasted_iota(jnp.int32, (1, n_new), 1)
    d_b = pos - kb_pos
    d_n = pos - kn_pos
    ok_b = (d_b >= 0) & (d_b < WINDOW) & (kb_pos >= 0)
    ok_n = (d_n >= 0) & (d_n < WINDOW)
    s_b = _mm(q, buf_ref[:LANES, :].astype(BF16)) - slope * d_b.astype(F32)
    s_n = _nt(q, new[:, :LANES].astype(BF16)) - slope * d_n.astype(F32)
    mx = jnp.maximum(jnp.max(jnp.where(ok_b, s_b, NEG), axis=1, keepdims=True),
                     jnp.max(jnp.where(ok_n, s_n, NEG), axis=1, keepdims=True))
    mx = jnp.where(mx > 0.5 * NEG, mx, 0.0)
    e_b = jnp.where(ok_b, jnp.exp(s_b - mx), 0.0)
    e_n = jnp.where(ok_n, jnp.exp(s_n - mx), 0.0)
    den = jnp.maximum(jnp.sum(e_b, axis=1, keepdims=True) + jnp.sum(e_n, axis=1, keepdims=True), 1e-30)
    o_ref[...] = (_nt((e_b / den).astype(BF16), buf_ref[LANES:, :].astype(BF16))
                  + _mm((e_n / den).astype(BF16), new[:, LANES:].astype(BF16)))


def _win_sample(slopes, q, win_buf_t, kv_new, ts, past):
    ns, rows, _ = q.shape
    return pl.pallas_call(
        functools.partial(_win_sample_kernel, ts=ts, past=past),
        grid=(ns,),
        in_specs=[pl.BlockSpec(memory_space=pltpu.SMEM),
                  pl.BlockSpec((None, rows, LANES), lambda b: (b, 0, 0)),
                  pl.BlockSpec((None,) + win_buf_t.shape[1:], lambda b: (b, 0, 0)),
                  pl.BlockSpec((None,) + kv_new.shape[1:], lambda b: (b, 0, 0))],
        out_specs=pl.BlockSpec((None, rows, LANES), lambda b: (b, 0, 0)),
        out_shape=jax.ShapeDtypeStruct((ns, rows, LANES), F32),
        compiler_params=_params("arbitrary"),
        name="win_sample",
    )(slopes, q, win_buf_t, kv_new)


def _merge_sample_kernel(cmp_ref, sel_ref, win_ref, gate_ref, o_ref):
    lane = lax.broadcasted_iota(jnp.int32, (1, LANES), 1)
    low = lane < D_HEAD
    for g in range(NSA_KV_HEADS):
        for b in range(NSA_REP // 2):
            col = g * NSA_REP * D_HEAD + b * LANES
            total = None
            for br, ref in enumerate((cmp_ref, sel_ref, win_ref)):
                first, second = ref[g, 2 * b], ref[g, 2 * b + 1]
                if g == 0:
                    val = jnp.where(low, first, pltpu.roll(second, D_HEAD, 1))
                else:
                    val = jnp.where(low, pltpu.roll(first, D_HEAD, 1), second)
                gate = gate_ref[:, br * NSA_WIDTH + col: br * NSA_WIDTH + col + LANES].astype(F32)
                total = gate * val if total is None else total + gate * val
            o_ref[:, col:col + LANES] = total.astype(o_ref.dtype)


def _merge_sample(o_cmp, o_sel, o_win, gates):
    rows = gates.shape[0]
    full = lambda a: pl.BlockSpec(a.shape, lambda i: (0,) * a.ndim)
    return pl.pallas_call(
        _merge_sample_kernel,
        grid=(1,),
        in_specs=[full(o_cmp), full(o_sel), full(o_win), full(gates)],
        out_specs=pl.BlockSpec((rows, NSA_WIDTH), lambda i: (0, 0)),
        out_shape=jax.ShapeDtypeStruct((rows, NSA_WIDTH), BF16),
        compiler_params=_params("arbitrary"),
        name="merge_sample",
    )(o_cmp, o_sel, o_win, gates)


def _dup_kv_cols(w):
    d = w.shape[0]
    return jnp.repeat(w.reshape(d, 2 * NSA_KV_HEADS, 1, D_HEAD), 2, axis=2).reshape(d, 2 * KV_WIDTH)


def _gate_cols(w):
    d = w.shape[0]
    per = w.reshape(d, NSA_HEADS, N_NSA_BRANCHES).transpose(0, 2, 1)
    return jnp.repeat(per[..., None], D_HEAD, axis=3).reshape(d, N_NSA_BRANCHES * NSA_WIDTH)


def _expand_q_cols(w):
    d = w.shape[0]
    place = jax.nn.one_hot(jnp.arange(NSA_HEADS) // NSA_REP, NSA_KV_HEADS, dtype=w.dtype)
    return (w.reshape(d, NSA_HEADS, 1, D_HEAD) * place[None, :, :, None]).reshape(d, NSA_HEADS * LANES)


def _pad_rows(a, rows):
    return jnp.pad(a, [(0, 0)] * (a.ndim - 2) + [(0, rows - a.shape[-2]), (0, 0)])


def _keys_on_lanes(a):
    b, s = a.shape[:2]
    return jnp.transpose(a, (0, 2, 3, 4, 1)).reshape(b, -1, s)


def _groups(spec):
    groups, c = [], 0
    for width, act, dts in spec:
        groups.append((c, width, act, dts))
        c += width
    return tuple(groups)


def _prompt_layer(x, mods, lw, slopes, n, t, final):
    (w_in, g_mix, g_mlp, g_final, cmp_w, w_br_sb, w_br_nsa, w_out, w_up, w_down) = lw
    sh1, sc1, gt1, sh2, sc2, gt2 = mods
    tm = min(256, t)
    w_cat = jnp.concatenate([
        w_in[:, _C_SBQ:_C_SBKV] * ATTN_SCALE, w_in[:, _C_SBKV:_C_NSQ], w_in[:, _C_NSQ:_C_CMP] * ATTN_SCALE,
        w_in[:, _C_CMP:_C_SEL], _dup_kv_cols(w_in[:, _C_SEL:_C_WIN]), _dup_kv_cols(w_in[:, _C_WIN:_C_GATE]),
        _gate_cols(w_in[:, _C_GATE:_C_GSB])], axis=1).astype(BF16)
    groups = _groups(((SB_WIDTH, None, (BF16,)), (2 * SB_WIDTH, None, (BF16,)), (NSA_WIDTH, None, (BF16,)),
                      (KV_WIDTH, None, (F32,)), (2 * KV_WIDTH, None, (BF16,)), (2 * KV_WIDTH, None, (BF16,)),
                      (N_NSA_BRANCHES * NSA_WIDTH, "sigmoid", (BF16,))))
    w_t = jnp.concatenate([w_in[:, _C_SBKV:_C_NSQ], w_in[:, _C_CMP:_C_GATE]], axis=1).T.astype(BF16)
    t_groups = ((0, 2 * SB_WIDTH), (2 * SB_WIDTH, KV_WIDTH), (2 * SB_WIDTH + KV_WIDTH, KV_WIDTH),
                (2 * SB_WIDTH + 2 * KV_WIDTH, KV_WIDTH))
    (sbq, sbkv_b, nsq, cmp_kv, sel_dup, win_dup, gates, sbkv_t, cmp_t, sel_t, win_t) = _project(
        x, g_mix, sc1, sh1, w_cat, groups, tm, t, w_t, t_groups)
    rows_first = lambda a, h: a.reshape(n, 2, h, D_HEAD, a.shape[-1]).transpose(0, 4, 1, 2, 3)
    states = (rows_first(sbkv_t, SB_HEADS), rows_first(cmp_t, NSA_KV_HEADS), rows_first(sel_t, NSA_KV_HEADS),
              rows_first(win_t[:, :, t - min(WINDOW, t):], NSA_KV_HEADS))
    o_sb = _sb_prompt(sbq, sbkv_b, n, t, min(256, t))
    cw = jnp.repeat(cmp_w.reshape(CMP_BLOCK, 2 * NSA_KV_HEADS), D_HEAD, axis=1)
    o_nsa = _nsa_prompt(slopes, nsq, cmp_kv, cw, sel_dup, win_dup, gates, n, t, LANES)
    x1 = _mix(x, g_mix, sc1, sh1, gt1, o_sb, o_nsa, w_in[:, _C_GSB:].astype(BF16), w_br_sb.astype(BF16),
              w_br_nsa.astype(BF16), w_out.astype(BF16), tm, t)
    x2 = _mlp(x1, g_mlp, sc2, sh2, gt2, g_final, w_up.astype(BF16), w_down.astype(BF16), tm, t, final)
    return x2, states


def _sample_mixers(sbq, sbkv_b, nsqx, sel_kv, win_kv, gates, cmp_w, slopes, caches, page_table, ns, ts):
    cache_sb, cache_cmp, cache_sel, win_buf = caches
    m = ns * ts
    n_pages = page_table.shape[1]
    past = n_pages * PAGE_SIZE
    g_, r_ = NSA_KV_HEADS, NSA_REP
    q_rep = jnp.broadcast_to(sbq.reshape(ns, ts, 1, SB_WIDTH), (ns, ts, SB_HEADS, SB_WIDTH))
    q_rep = q_rep.reshape(ns, ts * SB_HEADS, SB_WIDTH)
    kv_new = _pad_rows(sbkv_b.reshape(ns, ts, 2 * SB_WIDTH), PAGE_SIZE)
    o_sb = _sb_sample(page_table, q_rep, kv_new, _keys_on_lanes(cache_sb), math.gcd(n_pages, 16)).reshape(m, SB_WIDTH)

    qx = nsqx.reshape(ns, ts, g_, r_, LANES)
    q_cmp = _pad_rows(qx.transpose(0, 3, 2, 1, 4).reshape(ns, r_, g_ * ts, LANES), 16)
    q_sel = jnp.stack([qx[:, :, g, :, g * D_HEAD:(g + 1) * D_HEAD] for g in range(g_)], axis=1)
    q_sel = _pad_rows(q_sel, 16)
    q_win = qx.transpose(0, 2, 3, 1, 4).reshape(ns, NSA_HEADS * ts, LANES)
    cw_t = jnp.tile(jnp.repeat(cmp_w.reshape(CMP_BLOCK, 2 * g_).T, D_HEAD, axis=0), (1, PAGE_SIZE // CMP_BLOCK))
    o_cmp, idx = _cmp_sample(page_table, slopes, q_cmp, cw_t, _keys_on_lanes(cache_cmp), math.gcd(n_pages, 32), ts)
    kv_rows = lambda a: _pad_rows(a.reshape(ns, ts, KV_WIDTH), 16)
    sel_new = _pad_rows(sel_kv.reshape(ns, ts, 2, g_, D_HEAD).transpose(0, 3, 2, 1, 4), 16)
    sel_slabs = _keys_on_lanes(cache_sel).reshape(-1, 2, g_, D_HEAD, PAGE_SIZE)
    o_sel = _sel_sample(page_table, idx[:, :, :g_ * ts], slopes, q_sel, sel_new, sel_slabs, ts)
    o_win = _win_sample(slopes, q_win, _keys_on_lanes(win_buf), kv_rows(win_kv), ts, past)
    oc = o_cmp[:, :, :g_ * ts].reshape(ns, r_, g_, ts, LANES).transpose(2, 1, 0, 3, 4).reshape(g_, r_, m, LANES)
    os_ = o_sel[:, :, :, :r_].transpose(1, 3, 0, 2, 4).reshape(g_, r_, m, D_HEAD)
    os_ = jnp.stack([jnp.pad(os_[g], ((0, 0), (0, 0), (g * D_HEAD, (g_ - 1 - g) * D_HEAD))) for g in range(g_)])
    ow = o_win.reshape(ns, g_, r_, ts, LANES).transpose(1, 2, 0, 3, 4).reshape(g_, r_, m, LANES)
    return o_sb, _merge_sample(oc, os_, ow, gates)


def _sample_layer(x, mods, lw, slopes, caches, page_table, ns, ts, final):
    (w_in, g_mix, g_mlp, g_final, cmp_w, w_br_sb, w_br_nsa, w_out, w_up, w_down) = lw
    sh1, sc1, gt1, sh2, sc2, gt2 = mods
    tm = sc1.shape[1]
    w_cat = jnp.concatenate([
        w_in[:, _C_SBQ:_C_SBKV] * ATTN_SCALE, w_in[:, _C_SBKV:_C_NSQ],
        _expand_q_cols(w_in[:, _C_NSQ:_C_CMP] * ATTN_SCALE), w_in[:, _C_CMP:_C_GATE],
        _gate_cols(w_in[:, _C_GATE:_C_GSB])], axis=1).astype(BF16)
    groups = _groups(((SB_WIDTH, None, (BF16,)), (2 * SB_WIDTH, None, (F32, BF16)), (NSA_HEADS * LANES, None, (BF16,)),
                      (KV_WIDTH, None, (F32,)), (KV_WIDTH, None, (F32,)), (KV_WIDTH, None, (F32,)),
                      (N_NSA_BRANCHES * NSA_WIDTH, "sigmoid", (BF16,))))
    sbq, sbkv, sbkv_b, nsqx, cmp_kv, sel_kv, win_kv, gates = _project(x, g_mix, sc1, sh1, w_cat, groups, tm, ts)
    o_sb, o_nsa = _sample_mixers(sbq, sbkv_b, nsqx, sel_kv, win_kv, gates, cmp_w, slopes, caches, page_table, ns, ts)
    x1 = _mix(x, g_mix, sc1, sh1, gt1, o_sb, o_nsa, w_in[:, _C_GSB:].astype(BF16), w_br_sb.astype(BF16),
              w_br_nsa.astype(BF16), w_out.astype(BF16), tm, ts)
    x2 = _mlp(x1, g_mlp, sc2, sh2, gt2, g_final, w_up.astype(BF16), w_down.astype(BF16), tm, ts, final)
    return x2, (sbkv, cmp_kv, sel_kv, win_kv)


def kernel(x_prompt, x_sample, cache_sb_kv, cache_cmp_kv, cache_sel_kv, state_win_kv, page_table, c_prompt, c_sample,
           w_ada, b_ada, g_mix, g_mlp, g_final, w_in, cmp_w, w_br_sb, w_br_nsa, w_out, w_up, w_down):
    depth = w_in.shape[0]
    n, t, d = x_prompt.shape
    ns, ts, _ = x_sample.shape
    slopes = jnp.exp2(-8.0 * (jnp.arange(NSA_HEADS, dtype=F32) + 1.0) / NSA_HEADS)
    xp = x_prompt.reshape(n * t, d)
    xs = x_sample.reshape(ns * ts, d)
    ms = ns * ts
    tms = min(LANES, ms)
    c_all = jnp.concatenate([c_prompt, c_sample], axis=0)
    st_p = ([], [], [], [])
    st_s = ([], [], [], [])
    for l in range(depth):
        final = l == depth - 1
        lw = (w_in[l], g_mix[l].reshape(1, d), g_mlp[l].reshape(1, d), g_final.reshape(1, d), cmp_w[l],
              w_br_sb[l], w_br_nsa[l], w_out[l], w_up[l], w_down[l])
        mod = _adaln(c_all, w_ada[l], b_ada[l])
        mods_p = [mod[:n, k * d:(k + 1) * d].reshape(n, 1, d) for k in range(6)]
        mods_s = [jnp.repeat(mod[n:, k * d:(k + 1) * d], ts, axis=0).reshape(ms // tms, tms, d) for k in range(6)]
        xp, sp = _prompt_layer(xp, mods_p, lw, slopes, n, t, final)
        caches = (cache_sb_kv[l], cache_cmp_kv[l], cache_sel_kv[l], state_win_kv[l])
        xs, ss = _sample_layer(xs, mods_s, lw, slopes, caches, page_table, ns, ts, final)
        kv_s = (ns, ts, 2, NSA_KV_HEADS, D_HEAD)
        win_new = jnp.concatenate([state_win_kv[l], ss[3].reshape(kv_s)], axis=1)
        ss = (ss[0].reshape(ns, ts, 2, SB_HEADS, D_HEAD), ss[1].reshape(kv_s), ss[2].reshape(kv_s),
              win_new[:, win_new.shape[1] - min(WINDOW, win_new.shape[1]):])
        for acc, s in zip(st_p, sp):
            acc.append(s)
        for acc, s in zip(st_s, ss):
            acc.append(s)
    return (xp.reshape(n, t, d), xs.reshape(ns, ts, d), jnp.stack(st_p[0]), jnp.stack(st_s[0]),
            jnp.stack(st_p[1]), jnp.stack(st_s[1]), jnp.stack(st_p[2]), jnp.stack(st_s[2]),
            jnp.stack(st_p[3]), jnp.stack(st_s[3]))
```

```python
import functools
import math

import jax
import jax.numpy as jnp
from jax import lax
from jax.experimental import pallas as pl
from jax.experimental.pallas import tpu as pltpu

F32 = jnp.float32
BF16 = jnp.bfloat16

D_HEAD = 64
SB_HEADS = 8
NSA_HEADS = 8
NSA_KV_HEADS = 2
NSA_REP = NSA_HEADS // NSA_KV_HEADS
SB_WIDTH = SB_HEADS * D_HEAD
NSA_WIDTH = NSA_HEADS * D_HEAD
KV_WIDTH = 2 * NSA_KV_HEADS * D_HEAD
N_NSA_BRANCHES = 3
CMP_BLOCK = 32
SEL_BLOCK = 64
TOP_N = 16
WINDOW = 512
PAGE_SIZE = 128
RMS_EPS = 1e-6
FORCE_SCORE = 1e4
ATTN_SCALE = 0.125
LANES = 128
NEG = -1e30
SB_GROUP = 4
KEY_GROUP = 256
VMEM_LIMIT = 56 * 1024 * 1024

_C_SBQ = 0
_C_SBKV = _C_SBQ + SB_WIDTH
_C_NSQ = _C_SBKV + 2 * SB_WIDTH
_C_CMP = _C_NSQ + NSA_WIDTH
_C_SEL = _C_CMP + KV_WIDTH
_C_WIN = _C_SEL + KV_WIDTH
_C_GATE = _C_WIN + KV_WIDTH
_C_GSB = _C_GATE + N_NSA_BRANCHES * NSA_HEADS


def _params(*sem):
    return pltpu.CompilerParams(dimension_semantics=sem, vmem_limit_bytes=VMEM_LIMIT)


def _nt(a, b):
    return lax.dot_general(a, b, (((1,), (1,)), ((), ())), preferred_element_type=F32)


def _mm(a, b):
    return jnp.dot(a, b, preferred_element_type=F32)


def _modulated_norm(x, g, sc, sh):
    y = x * lax.rsqrt(jnp.mean(x * x, axis=-1, keepdims=True) + RMS_EPS)
    return (y * g) * (1.0 + sc) + sh


def _adaln_kernel(c_ref, w_ref, b_ref, o_ref):
    c = c_ref[...]
    s = c * jax.nn.sigmoid(c)
    o_ref[...] = _mm(s.astype(BF16), w_ref[...].astype(BF16)) + b_ref[...]


def _adaln(c, w_ada, b_ada):
    n, d = c.shape
    n_out = w_ada.shape[1]
    tn = 512
    return pl.pallas_call(
        _adaln_kernel,
        grid=(n_out // tn,),
        in_specs=[pl.BlockSpec((n, d), lambda j: (0, 0)),
                  pl.BlockSpec((d, tn), lambda j: (0, j)),
                  pl.BlockSpec((1, tn), lambda j: (0, j))],
        out_specs=pl.BlockSpec((n, tn), lambda j: (0, j)),
        out_shape=jax.ShapeDtypeStruct((n, n_out), F32),
        compiler_params=_params("arbitrary"),
        name="adaln",
    )(c, w_ada, b_ada.reshape(1, n_out))


def _proj_kernel(x_ref, g_ref, sc_ref, sh_ref, w_ref, *rest, groups, t_groups):
    wt_ref, out_refs = (rest[0], rest[1:]) if t_groups else (None, rest)
    hb = _modulated_norm(x_ref[...], g_ref[...], sc_ref[...], sh_ref[...]).astype(BF16)
    k = 0
    for start, width, act, dtypes in groups:
        z = _mm(hb, w_ref[:, start:start + width])
        if act == "sigmoid":
            z = jax.nn.sigmoid(z)
        for dt in dtypes:
            out_refs[k][...] = z.astype(dt)
            k += 1
    for start, width in t_groups:
        out_refs[k][...] = _nt(wt_ref[start:start + width, :], hb)
        k += 1


def _mod_spec(mod, tm, rows_per_seq):
    arr, k, d = mod
    if arr.shape[1] == 1:
        tiles = rows_per_seq // tm
        return pl.BlockSpec((None, 1, d), lambda i: (i // tiles, 0, k))
    return pl.BlockSpec((None, tm, d), lambda i: (i, 0, k))


def _project(x, g, sc, sh, w_cat, groups, tm, rows_per_seq, w_t=None, t_groups=()):
    m, d = x.shape
    out_shape, out_specs = [], []
    for _, width, _, dtypes in groups:
        for dt in dtypes:
            out_shape.append(jax.ShapeDtypeStruct((m, width), dt))
            out_specs.append(pl.BlockSpec((tm, width), lambda i: (i, 0)))
    tiles = rows_per_seq // tm
    for _, width in t_groups:
        out_shape.append(jax.ShapeDtypeStruct((m // rows_per_seq, width, rows_per_seq), F32))
        out_specs.append(pl.BlockSpec((None, width, tm), lambda i: (i // tiles, 0, lax.rem(i, tiles))))
    operands = [x, g, sc[0], sh[0], w_cat]
    in_specs = [pl.BlockSpec((tm, d), lambda i: (i, 0)),
                pl.BlockSpec((1, d), lambda i: (0, 0)),
                _mod_spec(sc, tm, rows_per_seq), _mod_spec(sh, tm, rows_per_seq),
                pl.BlockSpec(w_cat.shape, lambda i: (0, 0))]
    if t_groups:
        operands.append(w_t)
        in_specs.append(pl.BlockSpec(w_t.shape, lambda i: (0, 0)))
    return pl.pallas_call(
        functools.partial(_proj_kernel, groups=groups, t_groups=tuple(t_groups)),
        grid=(m // tm,),
        in_specs=in_specs,
        out_specs=out_specs,
        out_shape=out_shape,
        compiler_params=_params("arbitrary"),
        name="in_proj",
    )(*operands)


def _mix_kernel(x_ref, g_ref, sc_ref, sh_ref, gt_ref, osb_ref, onsa_ref, wg_ref, wbs_ref, wbn_ref, wo_ref, o_ref):
    x = x_ref[...]
    d = x.shape[1]
    hb = _modulated_norm(x, g_ref[...], sc_ref[...], sh_ref[...]).astype(BF16)
    gate_sb = jax.nn.sigmoid(_mm(hb, wg_ref[:, :d]))
    gate_nsa = jax.nn.sigmoid(_mm(hb, wg_ref[:, d:]))
    mixed = gate_sb * _mm(osb_ref[...], wbs_ref[...]) + gate_nsa * _mm(onsa_ref[...], wbn_ref[...])
    o_ref[...] = x + gt_ref[...] * _mm(mixed.astype(BF16), wo_ref[...])


def _mix(x, g, sc, sh, gt, o_sb, o_nsa, w_gate, w_br_sb, w_br_nsa, w_out, tm, rows_per_seq):
    m, d = x.shape
    full = lambda a: pl.BlockSpec(a.shape, lambda i: (0, 0))
    return pl.pallas_call(
        _mix_kernel,
        grid=(m // tm,),
        in_specs=[pl.BlockSpec((tm, d), lambda i: (i, 0)), pl.BlockSpec((1, d), lambda i: (0, 0)),
                  _mod_spec(sc, tm, rows_per_seq), _mod_spec(sh, tm, rows_per_seq), _mod_spec(gt, tm, rows_per_seq),
                  pl.BlockSpec((tm, o_sb.shape[1]), lambda i: (i, 0)),
                  pl.BlockSpec((tm, o_nsa.shape[1]), lambda i: (i, 0)),
                  full(w_gate), full(w_br_sb), full(w_br_nsa), full(w_out)],
        out_specs=pl.BlockSpec((tm, d), lambda i: (i, 0)),
        out_shape=jax.ShapeDtypeStruct((m, d), F32),
        compiler_params=_params("arbitrary"),
        name="mix_out",
    )(x, g, sc[0], sh[0], gt[0], o_sb, o_nsa, w_gate, w_br_sb, w_br_nsa, w_out)


def _mlp_kernel(x_ref, g_ref, sc_ref, sh_ref, gt_ref, gf_ref, wu_ref, wd_ref, o_ref, *, final):
    x = x_ref[...]
    hb = _modulated_norm(x, g_ref[...], sc_ref[...], sh_ref[...]).astype(BF16)
    u = jnp.square(jnp.maximum(_mm(hb, wu_ref[...]), 0.0))
    x2 = x + gt_ref[...] * _mm(u.astype(BF16), wd_ref[...])
    if final:
        x2 = x2 * lax.rsqrt(jnp.mean(x2 * x2, axis=-1, keepdims=True) + RMS_EPS) * gf_ref[...]
    o_ref[...] = x2


def _mlp(x, g, sc, sh, gt, g_final, w_up, w_down, tm, rows_per_seq, final):
    m, d = x.shape
    full = lambda a: pl.BlockSpec(a.shape, lambda i: (0, 0))
    return pl.pallas_call(
        functools.partial(_mlp_kernel, final=final),
        grid=(m // tm,),
        in_specs=[pl.BlockSpec((tm, d), lambda i: (i, 0)), pl.BlockSpec((1, d), lambda i: (0, 0)),
                  _mod_spec(sc, tm, rows_per_seq), _mod_spec(sh, tm, rows_per_seq), _mod_spec(gt, tm, rows_per_seq),
                  pl.BlockSpec((1, d), lambda i: (0, 0)), full(w_up), full(w_down)],
        out_specs=pl.BlockSpec((tm, d), lambda i: (i, 0)),
        out_shape=jax.ShapeDtypeStruct((m, d), F32),
        compiler_params=_params("arbitrary"),
        name="mlp_final",
    )(x, g, sc[0], sh[0], gt[0], g_final, w_up, w_down)


def _log_sigmoids(z):
    lsz = jnp.minimum(z, 0.0) - jnp.log(1.0 + jnp.exp(-jnp.abs(z)))
    return lsz, lsz - z


def _split_bf16(x):
    hi = x.astype(BF16)
    return hi, (x - hi.astype(F32)).astype(BF16)


def _split3_bf16(x):
    hi = x.astype(BF16)
    r1 = x - hi.astype(F32)
    mid = r1.astype(BF16)
    return hi, mid, (r1 - mid.astype(F32)).astype(BF16)


def _later_and_rowsum():
    r = lax.broadcasted_iota(jnp.int32, (2 * LANES, 2 * LANES), 0)
    c = lax.broadcasted_iota(jnp.int32, (2 * LANES, 2 * LANES), 1)
    return jnp.where((c >= LANES) | ((r & (LANES - 1)) > c), 1.0, 0.0).astype(BF16)


def _sb_group(z, msk, pv, later, acc_ref, carry_ref, latest_first):
    rows = z.shape[0]
    nb = z.shape[1] // LANES
    lsz, l1m = _log_sigmoids(z)
    if msk is not None:
        l1m = jnp.where(msk, l1m, 0.0)
    hi, lo = _split_bf16(l1m)
    stack = lambda x: jnp.concatenate([x[:, u * LANES:(u + 1) * LANES] for u in range(nb)], axis=0)
    cs = _mm(jnp.concatenate([stack(hi), stack(lo)], axis=1), later)
    carry = carry_ref[...]
    parts = [None] * nb
    for u in (range(nb) if latest_first else reversed(range(nb))):
        c_u = cs[u * rows:(u + 1) * rows]
        parts[u] = lsz[:, u * LANES:(u + 1) * LANES] + c_u[:, :LANES] + carry
        carry = carry + c_u[:, LANES:]
    carry_ref[...] = carry
    a = jnp.exp(jnp.concatenate(parts, axis=1))
    if msk is not None:
        a = jnp.where(msk, a, 0.0)
    acc_ref[...] += pv(a.astype(BF16))


def _sb_prompt_kernel(q_ref, k_ref, v_ref, o_ref, acc_ref, carry_ref, *, tq):
    i = pl.program_id(2)
    nb = tq // LANES
    lane = lax.broadcasted_iota(jnp.int32, (1, LANES), 1)
    q = q_ref[...]
    zero = jnp.zeros_like(q)
    qs = jnp.concatenate([jnp.where(lane < D_HEAD, q, zero), jnp.where(lane >= D_HEAD, q, zero)], axis=0)
    later = _later_and_rowsum()
    q_pos = i * tq + lax.rem(lax.broadcasted_iota(jnp.int32, (2 * tq, 1), 0), tq)
    acc_ref[...] = jnp.zeros_like(acc_ref)
    carry_ref[...] = jnp.zeros_like(carry_ref)

    def group(j0, n_tiles, masked):
        off = pl.multiple_of(j0 * LANES, LANES)
        kb = k_ref[pl.ds(off, n_tiles * LANES), :]
        vb = v_ref[pl.ds(off, n_tiles * LANES), :]
        msk = None
        if masked:
            msk = (off + lax.broadcasted_iota(jnp.int32, (1, n_tiles * LANES), 1)) < q_pos
        _sb_group(_nt(qs, kb), msk, lambda a: _mm(a, vb), later, acc_ref, carry_ref, latest_first=False)

    group(i * nb, nb, True)
    below = i * nb
    rem = lax.rem(below, SB_GROUP)
    for r in range(1, SB_GROUP):
        @pl.when(rem == r)
        def _(r=r):
            group(below - r, r, False)

    def body(jj, c):
        group(below - rem - SB_GROUP * (jj + 1), SB_GROUP, False)
        return c

    lax.fori_loop(0, below // SB_GROUP, body, 0)
    acc = acc_ref[...]
    o_ref[...] = jnp.where(lane < D_HEAD, acc[:tq], acc[tq:]).astype(o_ref.dtype)


def _sb_prompt(q, kv, n, t, tq):
    pairs = SB_WIDTH // LANES
    nq = t // tq
    return pl.pallas_call(
        functools.partial(_sb_prompt_kernel, tq=tq),
        grid=(n, pairs, nq),
        in_specs=[pl.BlockSpec((tq, LANES), lambda b, p, i: (b * nq + i, p)),
                  pl.BlockSpec((t, LANES), lambda b, p, i: (b, p)),
                  pl.BlockSpec((t, LANES), lambda b, p, i: (b, pairs + p))],
        out_specs=pl.BlockSpec((tq, LANES), lambda b, p, i: (b * nq + i, p)),
        out_shape=jax.ShapeDtypeStruct((n * t, SB_WIDTH), BF16),
        scratch_shapes=[pltpu.VMEM((2 * tq, LANES), F32), pltpu.VMEM((2 * tq, LANES), F32)],
        compiler_params=_params("arbitrary", "arbitrary", "arbitrary"),
        name="sb_prompt",
    )(q, kv, kv)


def _nsa_prompt_kernel(slopes_ref, q_ref, cmp_ref, cw_ref, ksel_ref, vsel_ref, kwin_ref, vwin_ref,
                       g0_ref, g1_ref, g2_ref, o_ref, ck_ref, s_ref, mrun_ref, acc_ref, osel_ref, *, tq, t_len):
    g = pl.program_id(1)
    i = pl.program_id(2)
    nsel = t_len // SEL_BLOCK
    rows = NSA_REP * tq
    lane = lax.broadcasted_iota(jnp.int32, (1, LANES), 1)
    low = lane < D_HEAD

    @pl.when(i == 0)
    def _compress():
        xr = cmp_ref[...].reshape(nsel, SEL_BLOCK, KV_WIDTH)
        cw = cw_ref[...][None]
        own = jnp.where(low, 0, 1) == g
        halves = (jnp.sum(xr[:, :CMP_BLOCK, :] * cw, axis=1), jnp.sum(xr[:, CMP_BLOCK:, :] * cw, axis=1))
        for e, c in enumerate(halves):
            for kv in range(2):
                part = c[:, kv * LANES:(kv + 1) * LANES]
                ck_ref[2 * kv + e] = jnp.where(own, part, pltpu.roll(part, D_HEAD, 1)).astype(BF16)

    q = q_ref[...]
    zero = jnp.zeros((tq, LANES), q.dtype)
    pieces = []
    for b in range(NSA_REP // 2):
        qb = q[:, b * LANES:(b + 1) * LANES]
        pieces += [jnp.where(low, qb, zero), jnp.where(low, zero, qb)]
    qs = jnp.concatenate(pieces, axis=0)
    slope = [slopes_ref[g * NSA_REP + r] for r in range(NSA_REP)]
    slope_col = jnp.concatenate([jnp.full((tq, 1), slope[r], F32) for r in range(NSA_REP)], axis=0)
    t_col = i * tq + lax.rem(lax.broadcasted_iota(jnp.int32, (rows, 1), 0), tq)

    kde, kdo, vde, vdo = ck_ref[0], ck_ref[1], ck_ref[2], ck_ref[3]

    jl = lax.broadcasted_iota(jnp.int32, (1, nsel), 1)
    end_e = jl * SEL_BLOCK + (CMP_BLOCK - 1)
    end_o = jl * SEL_BLOCK + (SEL_BLOCK - 1)
    s_e = _nt(qs, kde) - slope_col * (t_col - end_e).astype(F32)
    s_o = _nt(qs, kdo) - slope_col * (t_col - end_o).astype(F32)
    ok_e = end_e <= t_col
    ok_o = end_o <= t_col
    mx = jnp.maximum(jnp.max(jnp.where(ok_e, s_e, NEG), axis=1, keepdims=True),
                     jnp.max(jnp.where(ok_o, s_o, NEG), axis=1, keepdims=True))
    mx = jnp.where(mx > 0.5 * NEG, mx, 0.0)
    e_e = jnp.where(ok_e, jnp.exp(s_e - mx), 0.0)
    e_o = jnp.where(ok_o, jnp.exp(s_o - mx), 0.0)
    den = jnp.maximum(jnp.sum(e_e, axis=1, keepdims=True) + jnp.sum(e_o, axis=1, keepdims=True), 1e-30)
    o_cmp = _mm((e_e / den).astype(BF16), vde) + _mm((e_o / den).astype(BF16), vdo)

    jr = lax.broadcasted_iota(jnp.int32, (nsel, tq), 0)
    tl = i * tq + lax.broadcasted_iota(jnp.int32, (nsel, tq), 1)
    end_et = jr * SEL_BLOCK + (CMP_BLOCK - 1)
    end_ot = jr * SEL_BLOCK + (SEL_BLOCK - 1)
    ok_et = end_et <= tl
    ok_ot = end_ot <= tl
    st_e = _nt(kde, qs)
    st_o = _nt(kdo, qs)
    score = jnp.zeros((nsel, tq), F32)
    for r in range(NSA_REP):
        a_e = st_e[:, r * tq:(r + 1) * tq] - slope[r] * (tl - end_et).astype(F32)
        a_o = st_o[:, r * tq:(r + 1) * tq] - slope[r] * (tl - end_ot).astype(F32)
        mt = jnp.maximum(jnp.max(jnp.where(ok_et, a_e, NEG), axis=0, keepdims=True),
                         jnp.max(jnp.where(ok_ot, a_o, NEG), axis=0, keepdims=True))
        mt = jnp.where(mt > 0.5 * NEG, mt, 0.0)
        x_e = jnp.where(ok_et, jnp.exp(a_e - mt), 0.0)
        x_o = jnp.where(ok_ot, jnp.exp(a_o - mt), 0.0)
        dt = jnp.maximum(jnp.sum(x_e, axis=0, keepdims=True) + jnp.sum(x_o, axis=0, keepdims=True), 1e-30)
        score = score + (x_e / dt + x_o / dt)
    cur = tl // SEL_BLOCK
    forced = (jr == 0) | (jr == cur) | (jr == cur - 1)
    score = jnp.where(forced, FORCE_SCORE, jnp.where(jr * SEL_BLOCK <= tl, score, -1.0))
    rank = jnp.zeros((nsel, tq), jnp.int32)
    for j2 in range(nsel):
        other = score[j2:j2 + 1, :]
        ahead = (other > score) | ((other == score) & (jr > j2))
        rank = rank + jnp.where(ahead, 1, 0)
    sel_t = jnp.where(rank < min(TOP_N, nsel), 1.0, 0.0).astype(BF16)
    eye_r = lax.broadcasted_iota(jnp.int32, (tq, tq), 0)
    eye_c = lax.broadcasted_iota(jnp.int32, (tq, tq), 1)
    eye = jnp.where(eye_r == eye_c, 1.0, 0.0).astype(BF16)
    sel = _nt(eye, sel_t).astype(BF16)

    t0 = i * tq
    lane_g = lax.broadcasted_iota(jnp.int32, (1, KEY_GROUP), 1)
    row_minus_lane = (lax.broadcasted_iota(jnp.int32, (tq, KEY_GROUP), 0)
                      - lax.broadcasted_iota(jnp.int32, (tq, KEY_GROUP), 1))
    ones_v = jnp.ones((KEY_GROUP, LANES), BF16)
    g_hi = (t0 + tq + KEY_GROUP - 1) // KEY_GROUP

    def for_groups(g_lo, body):
        n = g_hi - g_lo

        def triple(jj, c):
            for u in range(3):
                body(g_lo + 3 * jj + u)
            return c

        lax.fori_loop(0, n // 3, triple, 0)
        rem = lax.rem(n, 3)
        for r in (1, 2):
            @pl.when(rem == r)
            def _(r=r):
                for u in range(r):
                    body(g_hi - r + u)

    def sweep(k_ref_, v_ref_, g_lo, mask_bias):
        mrun_ref[...] = jnp.full_like(mrun_ref, NEG)

        def pass1(gi):
            off = pl.multiple_of(gi * KEY_GROUP, KEY_GROUP)
            add = mask_bias(gi, row_minus_lane + (t0 - off))
            k_rel = (off - t0 + lane_g).astype(F32)
            raw = _nt(qs, k_ref_[pl.ds(off, KEY_GROUP), :])
            s = jnp.concatenate([raw[r * tq:(r + 1) * tq] + (add + slope[r] * k_rel) for r in range(NSA_REP)],
                                axis=0)
            s_ref[gi - g_lo] = s
            mrun_ref[...] = jnp.maximum(mrun_ref[...], s)

        for_groups(g_lo, pass1)
        mrun_ref[...] = jnp.broadcast_to(jnp.max(mrun_ref[...], axis=1, keepdims=True), mrun_ref.shape)
        acc_ref[...] = jnp.zeros_like(acc_ref)

        def pass2(gi):
            off = pl.multiple_of(gi * KEY_GROUP, KEY_GROUP)
            p = jnp.exp(s_ref[gi - g_lo] - mrun_ref[...]).astype(BF16)
            acc_ref[...] += _mm(p, jnp.concatenate([v_ref_[pl.ds(off, KEY_GROUP), :], ones_v], axis=1))

        for_groups(g_lo, pass2)
        acc = acc_ref[...]
        return acc[:, :LANES] / jnp.maximum(acc[:, LANES:], 1e-30)

    per_group = KEY_GROUP // SEL_BLOCK
    br = lax.broadcasted_iota(jnp.int32, (nsel, KEY_GROUP), 0)
    bc = lax.broadcasted_iota(jnp.int32, (nsel, KEY_GROUP), 1) // SEL_BLOCK
    not_sel = ((sel.astype(F32) - 1.0) * (-NEG)).astype(BF16)

    def sel_bias(gi, dist):
        expand = jnp.where(br == gi * per_group + bc, 1.0, 0.0).astype(BF16)
        return jnp.where(dist >= 0, _mm(not_sel, expand), NEG)

    osel_ref[...] = sweep(ksel_ref, vsel_ref, 0, sel_bias)

    def win_bias(gi, dist):
        return jnp.where((dist >= 0) & (dist < WINDOW), 0.0, NEG)

    o_win = sweep(kwin_ref, vwin_ref, jnp.maximum(t0 - (WINDOW - 1), 0) // KEY_GROUP, win_bias)
    o_sel = osel_ref[...]

    for b in range(NSA_REP // 2):
        lo_rows = slice(2 * b * tq, (2 * b + 1) * tq)
        hi_rows = slice((2 * b + 1) * tq, (2 * b + 2) * tq)
        cols = slice(b * LANES, (b + 1) * LANES)
        pick = lambda o: jnp.where(low, o[lo_rows], o[hi_rows])
        merged = (g0_ref[:, cols].astype(F32) * pick(o_cmp) + g1_ref[:, cols].astype(F32) * pick(o_sel)
                  + g2_ref[:, cols].astype(F32) * pick(o_win))
        o_ref[:, cols] = merged.astype(o_ref.dtype)


def _nsa_prompt(slopes, q, cmp_kv, cw, sel_dup, win_dup, gates, n, t, tq):
    nq = t // tq
    gw = NSA_REP * D_HEAD
    row = lambda b, g, i: b * nq + i
    seq = lambda c: pl.BlockSpec((t, LANES), lambda b, g, i, c=c: (b, c * NSA_KV_HEADS + g))
    gate = lambda br: pl.BlockSpec((tq, gw), lambda b, g, i, br=br: (row(b, g, i), br * NSA_KV_HEADS + g))
    nsel = t // SEL_BLOCK
    rows = NSA_REP * tq
    return pl.pallas_call(
        functools.partial(_nsa_prompt_kernel, tq=tq, t_len=t),
        grid=(n, NSA_KV_HEADS, nq),
        in_specs=[pl.BlockSpec(memory_space=pltpu.SMEM),
                  pl.BlockSpec((tq, gw), lambda b, g, i: (row(b, g, i), g)),
                  pl.BlockSpec((t, KV_WIDTH), lambda b, g, i: (b, 0)),
                  pl.BlockSpec((CMP_BLOCK, KV_WIDTH), lambda b, g, i: (0, 0)),
                  seq(0), seq(1), seq(0), seq(1), gate(0), gate(1), gate(2)],
        out_specs=pl.BlockSpec((tq, gw), lambda b, g, i: (row(b, g, i), g)),
        out_shape=jax.ShapeDtypeStruct((n * t, NSA_WIDTH), BF16),
        scratch_shapes=[pltpu.VMEM((4, nsel, LANES), BF16), pltpu.VMEM((t // KEY_GROUP, rows, KEY_GROUP), F32),
                        pltpu.VMEM((rows, KEY_GROUP), F32), pltpu.VMEM((rows, 2 * LANES), F32),
                        pltpu.VMEM((rows, LANES), F32)],
        compiler_params=_params("arbitrary", "arbitrary", "arbitrary"),
        name="nsa_prompt",
    )(slopes, q, cmp_kv, cw, sel_dup, sel_dup, win_dup, win_dup, gates, gates, gates)


def _sb_sample_kernel(pt_ref, q_ref, new_ref, *rest, n_step, ts):
    page_refs, o_ref, acc_ref, carry_ref = rest[:n_step], rest[n_step], rest[n_step + 1], rest[n_step + 2]
    jj = pl.program_id(1)
    rows = ts * SB_HEADS
    rr = lax.broadcasted_iota(jnp.int32, (rows, SB_WIDTH), 0)
    cc = lax.broadcasted_iota(jnp.int32, (rows, SB_WIDTH), 1)
    own_head = (cc // D_HEAD) == lax.rem(rr, SB_HEADS)
    q = q_ref[...]
    qrows = jnp.where(own_head, q, jnp.zeros_like(q))
    later = _later_and_rowsum()

    @pl.when(jj == 0)
    def _first():
        acc_ref[...] = jnp.zeros_like(acc_ref)
        carry_ref[...] = jnp.zeros_like(carry_ref)
        new = new_ref[...]
        s_key = lax.broadcasted_iota(jnp.int32, (rows, PAGE_SIZE), 1)
        t_qry = lax.broadcasted_iota(jnp.int32, (rows, PAGE_SIZE), 0) // SB_HEADS
        _sb_group(_nt(qrows, new[:, :SB_WIDTH]), s_key < t_qry, lambda a: _mm(a, new[:, SB_WIDTH:]),
                  later, acc_ref, carry_ref, latest_first=True)

    k_t = jnp.concatenate([p[:SB_WIDTH, :] for p in page_refs], axis=1).astype(BF16)
    v_t = jnp.concatenate([p[SB_WIDTH:, :] for p in page_refs], axis=1).astype(BF16)
    _sb_group(_mm(qrows, k_t), None, lambda a: _nt(a, v_t), later, acc_ref, carry_ref, latest_first=True)

    @pl.when(jj == pl.num_programs(1) - 1)
    def _last():
        own = jnp.where(own_head, acc_ref[...], 0.0)
        o_ref[...] = jnp.sum(own.reshape(ts, SB_HEADS, SB_WIDTH), axis=1).astype(o_ref.dtype)


def _sb_sample(page_table, q_rep, kv_new, cache, n_step):
    ns, rows, _ = q_rep.shape
    ts = rows // SB_HEADS
    n_pages = page_table.shape[1]
    steps = n_pages // n_step

    def page_spec(u):
        def index(b, jj, pt):
            return (pt[b * n_pages + (n_pages - 1 - (jj * n_step + u))], 0, 0)
        return pl.BlockSpec((None, 2 * SB_WIDTH, PAGE_SIZE), index)

    grid_spec = pltpu.PrefetchScalarGridSpec(
        num_scalar_prefetch=1,
        grid=(ns, steps),
        in_specs=[pl.BlockSpec((None, rows, SB_WIDTH), lambda b, jj, pt: (b, 0, 0)),
                  pl.BlockSpec((None, PAGE_SIZE, 2 * SB_WIDTH), lambda b, jj, pt: (b, 0, 0))]
                 + [page_spec(u) for u in range(n_step)],
        out_specs=pl.BlockSpec((None, ts, SB_WIDTH), lambda b, jj, pt: (b, 0, 0)),
        scratch_shapes=[pltpu.VMEM((rows, SB_WIDTH), F32), pltpu.VMEM((rows, LANES), F32)])
    return pl.pallas_call(
        functools.partial(_sb_sample_kernel, n_step=n_step, ts=ts),
        grid_spec=grid_spec,
        out_shape=jax.ShapeDtypeStruct((ns, ts, SB_WIDTH), BF16),
        compiler_params=_params("arbitrary", "arbitrary"),
        name="sb_sample",
    )(page_table.reshape(-1), q_rep, kv_new, *([cache] * n_step))


def _pool_matrices(n_pages, n_step):
    steps = n_pages // n_step
    nps = n_pages * PAGE_SIZE // SEL_BLOCK
    key = jnp.arange(steps * n_step * PAGE_SIZE)
    col = (key // CMP_BLOCK % 2) * nps + key // SEL_BLOCK
    return jax.nn.one_hot(col, 2 * nps, dtype=BF16).reshape(steps, n_step * PAGE_SIZE, 2 * nps)


def _cmp_sample_kernel(pt_ref, slopes_ref, q_ref, cw_ref, pool_ref, *rest, n_step, ts, past):
    page_refs = rest[:n_step]
    o_ref, idx_ref, ck_ref = rest[n_step:]
    jj = pl.program_id(1)
    nps = past // SEL_BLOCK
    rows = q_ref.shape[1]
    n_cols = NSA_KV_HEADS * ts

    @pl.when(jj == 0)
    def _zero():
        ck_ref[...] = jnp.zeros_like(ck_ref)

    cw = cw_ref[...]
    weighted = jnp.concatenate([r[...] * cw for r in page_refs], axis=1)
    pooled = _mm(jnp.concatenate(_split_bf16(weighted), axis=0), pool_ref[jj])
    ck_ref[...] += pooled[:KV_WIDTH] + pooled[KV_WIDTH:]

    @pl.when(jj == pl.num_programs(1) - 1)
    def _attend():
        ck = ck_ref[...]
        ke_t, ko_t = ck[:LANES, :nps].astype(BF16), ck[:LANES, nps:].astype(BF16)
        ve_t, vo_t = ck[LANES:, :nps].astype(BF16), ck[LANES:, nps:].astype(BF16)
        jl = lax.broadcasted_iota(jnp.int32, (rows, nps), 1)
        rl = lax.broadcasted_iota(jnp.int32, (rows, nps), 0)
        pos_n = past + lax.rem(rl, ts)
        d_e = pos_n - (jl * SEL_BLOCK + CMP_BLOCK - 1)
        d_o = pos_n - (jl * SEL_BLOCK + SEL_BLOCK - 1)
        p_sum = jnp.zeros((rows, nps), F32)
        for r in range(NSA_REP):
            q = q_ref[r]
            sl = [slopes_ref[g * NSA_REP + r] for g in range(NSA_KV_HEADS)]
            slope = jnp.where(rl // ts == 0, sl[0], sl[1])
            s_e = _mm(q, ke_t) - slope * d_e.astype(F32)
            s_o = _mm(q, ko_t) - slope * d_o.astype(F32)
            mx = jnp.maximum(jnp.max(jnp.where(d_e >= 0, s_e, NEG), axis=1, keepdims=True),
                             jnp.max(jnp.where(d_o >= 0, s_o, NEG), axis=1, keepdims=True))
            mx = jnp.where(mx > 0.5 * NEG, mx, 0.0)
            e_e = jnp.where(d_e >= 0, jnp.exp(s_e - mx), 0.0)
            e_o = jnp.where(d_o >= 0, jnp.exp(s_o - mx), 0.0)
            den = jnp.maximum(jnp.sum(e_e, axis=1, keepdims=True) + jnp.sum(e_o, axis=1, keepdims=True), 1e-30)
            p_e, p_o = e_e / den, e_o / den
            o_ref[r] = _nt(p_e.astype(BF16), ve_t) + _nt(p_o.astype(BF16), vo_t)
            p_sum = p_sum + (p_e + p_o)
        cur = pos_n // SEL_BLOCK
        forced = (jl == 0) | (jl == cur) | (jl == cur - 1)
        score = jnp.where(forced, FORCE_SCORE, jnp.where(jl * SEL_BLOCK <= pos_n, p_sum, -1.0))
        pos_1 = past + lax.rem(lax.broadcasted_iota(jnp.int32, (rows, 1), 0), ts)
        cur_1 = pos_1 // SEL_BLOCK
        forced_1 = (cur_1 == nps) | (cur_1 - 1 == nps)
        tail = jnp.where(forced_1, FORCE_SCORE, jnp.where(nps * SEL_BLOCK <= pos_1, 0.0, -1.0))
        er = lax.broadcasted_iota(jnp.int32, (nps, nps), 0)
        ec = lax.broadcasted_iota(jnp.int32, (nps, nps), 1)
        eye = jnp.where(er == ec, 1.0, 0.0).astype(BF16)
        score_t = sum(_nt(eye, piece) for piece in _split3_bf16(score))
        k_f = lax.broadcasted_iota(jnp.int32, (TOP_N, nps), 0).astype(F32)
        j_f = lax.broadcasted_iota(jnp.int32, (TOP_N, nps), 1).astype(F32)
        k_col = lax.broadcasted_iota(jnp.int32, (TOP_N, 1), 0).astype(F32)
        c_lane = lax.broadcasted_iota(jnp.int32, (TOP_N, rows), 1)
        picked = jnp.zeros((TOP_N, rows), F32)
        for c in range(n_cols):
            other = score_t[:, c:c + 1]
            mine = score[c:c + 1, :]
            tail_c = tail[c:c + 1, :]
            ahead = (other > mine) | ((other == mine) & (er < ec))
            rank = jnp.sum(jnp.where(ahead, 1.0, 0.0), axis=0, keepdims=True) + jnp.where(tail_c > mine, 1.0, 0.0)
            rank_tail = jnp.sum(jnp.where(mine >= tail_c, 1.0, 0.0), axis=1, keepdims=True)
            hit = jnp.sum(jnp.where(rank == k_f, j_f, 0.0), axis=1, keepdims=True)
            hit = hit + jnp.where(rank_tail == k_col, float(nps), 0.0)
            picked = jnp.where(c_lane == c, hit, picked)
        idx_ref[...] = picked.astype(jnp.int32)


def _cmp_sample(page_table, slopes, q, cw_t, cache, n_step, ts):
    ns, _, rows, _ = q.shape
    n_pages = page_table.shape[1]
    steps = n_pages // n_step
    past = n_pages * PAGE_SIZE
    nps = past // SEL_BLOCK
    pool = _pool_matrices(n_pages, n_step)

    def page_spec(u):
        return pl.BlockSpec((None, KV_WIDTH, PAGE_SIZE),
                            lambda b, jj, pt: (pt[b * n_pages + jj * n_step + u], 0, 0))

    grid_spec = pltpu.PrefetchScalarGridSpec(
        num_scalar_prefetch=1,
        grid=(ns, steps),
        in_specs=[pl.BlockSpec(memory_space=pltpu.SMEM),
                  pl.BlockSpec((None, NSA_REP, rows, LANES), lambda b, jj, pt: (b, 0, 0, 0)),
                  pl.BlockSpec(cw_t.shape, lambda b, jj, pt: (0, 0)),
                  pl.BlockSpec(pool.shape, lambda b, jj, pt: (0, 0, 0))]
                 + [page_spec(u) for u in range(n_step)],
        out_specs=[pl.BlockSpec((None, NSA_REP, rows, LANES), lambda b, jj, pt: (b, 0, 0, 0)),
                   pl.BlockSpec((None, TOP_N, rows), lambda b, jj, pt: (b, 0, 0))],
        scratch_shapes=[pltpu.VMEM((KV_WIDTH, 2 * nps), F32)])
    return pl.pallas_call(
        functools.partial(_cmp_sample_kernel, n_step=n_step, ts=ts, past=past),
        grid_spec=grid_spec,
        out_shape=[jax.ShapeDtypeStruct((ns, NSA_REP, rows, LANES), F32),
                   jax.ShapeDtypeStruct((ns, TOP_N, rows), jnp.int32)],
        compiler_params=_params("arbitrary", "arbitrary"),
        name="cmp_sample",
    )(page_table.reshape(-1), slopes, q, cw_t, pool, *([cache] * n_step))


def _sel_sample_kernel(page_ref, idx_ref, slopes_ref, q_ref, new_ref, *rest, ts, past, cols):
    all_refs, o_ref = rest[:ts * TOP_N], rest[ts * TOP_N]
    b, g = pl.program_id(0), pl.program_id(1)
    rows = q_ref.shape[1]
    rr = lax.broadcasted_iota(jnp.int32, (rows, 1), 0)
    slope = jnp.zeros((rows, 1), F32)
    for r in range(NSA_REP):
        slope = jnp.where(rr == r, slopes_ref[g * NSA_REP + r], slope)
    for t in range(ts):
        o_ref[t] = _sel_one_query(idx_ref, q_ref[t], new_ref, all_refs[t * TOP_N:(t + 1) * TOP_N], slope,
                                  (b * TOP_N) * cols + g * ts + t, cols, past, t)


def _sel_one_query(idx_ref, q, new_ref, kv_refs, slope, idx_base, cols, past, t):
    nps = past // SEL_BLOCK
    per_page = PAGE_SIZE // SEL_BLOCK
    width = TOP_N * PAGE_SIZE
    pos = past + t
    its = [idx_ref[idx_base + k * cols] for k in range(TOP_N)]
    ll = lax.broadcasted_iota(jnp.int32, (1, width), 1)
    block_of = jnp.zeros((1, width), jnp.int32)
    for k in range(TOP_N):
        block_of = jnp.where(ll // PAGE_SIZE == k, its[k], block_of)
    in_page = lax.rem(ll, PAGE_SIZE)
    k_pos = block_of * SEL_BLOCK + lax.rem(in_page, SEL_BLOCK)
    ok = (block_of < nps) & (in_page // SEL_BLOCK == lax.rem(block_of, per_page)) & (k_pos <= pos)
    k_t = jnp.concatenate([r[0] for r in kv_refs], axis=1).astype(BF16)
    v_t = jnp.concatenate([r[1] for r in kv_refs], axis=1).astype(BF16)
    s = _mm(q, k_t) - slope * (pos - k_pos).astype(F32)
    n_sel_new = jnp.where(its[0] >= nps, 1, 0)
    for k in range(1, TOP_N):
        n_sel_new = n_sel_new + jnp.where(its[k] >= nps, 1, 0)
    limit = jnp.where(n_sel_new > 0, pos, past - 1)
    n_new = new_ref.shape[1]
    kn_pos = past + lax.broadcasted_iota(jnp.int32, (1, n_new), 1)
    ok_n = kn_pos <= limit
    k_n = new_ref[0].astype(BF16)
    v_n = new_ref[1].astype(BF16)
    s_n = _nt(q, k_n) - slope * (pos - kn_pos).astype(F32)
    mx = jnp.maximum(jnp.max(jnp.where(ok, s, NEG), axis=1, keepdims=True),
                     jnp.max(jnp.where(ok_n, s_n, NEG), axis=1, keepdims=True))
    mx = jnp.where(mx > 0.5 * NEG, mx, 0.0)
    e = jnp.where(ok, jnp.exp(s - mx), 0.0)
    e_n = jnp.where(ok_n, jnp.exp(s_n - mx), 0.0)
    den = jnp.maximum(jnp.sum(e, axis=1, keepdims=True) + jnp.sum(e_n, axis=1, keepdims=True), 1e-30)
    return _nt((e / den).astype(BF16), v_t) + _mm((e_n / den).astype(BF16), v_n)


def _sel_sample(page_table, idx, slopes, q, kv_new, cache, ts):
    ns = q.shape[0]
    n_pages = page_table.shape[1]
    past = n_pages * PAGE_SIZE
    nps = past // SEL_BLOCK
    per_page = PAGE_SIZE // SEL_BLOCK
    cols = idx.shape[2]
    rows = q.shape[3]
    logical = jnp.minimum(idx, nps - 1) // per_page
    hot = (logical[..., None] == jnp.arange(n_pages)).astype(jnp.int32)
    pages = jnp.sum(hot * page_table[:, None, None, :], axis=-1)

    def slab_spec(t, k):
        def index(b, g, pg, ix):
            return (pg[(b * TOP_N + k) * cols + g * ts + t], 0, g, 0, 0)
        return pl.BlockSpec((None, 2, None, D_HEAD, PAGE_SIZE), index)

    grid_spec = pltpu.PrefetchScalarGridSpec(
        num_scalar_prefetch=2,
        grid=(ns, NSA_KV_HEADS),
        in_specs=[pl.BlockSpec(memory_space=pltpu.SMEM),
                  pl.BlockSpec((None, None, ts, rows, D_HEAD), lambda b, g, pg, ix: (b, g, 0, 0, 0)),
                  pl.BlockSpec((None, None) + kv_new.shape[2:], lambda b, g, pg, ix: (b, g, 0, 0, 0))]
                 + [slab_spec(t, k) for t in range(ts) for k in range(TOP_N)],
        out_specs=pl.BlockSpec((None, None, ts, rows, D_HEAD), lambda b, g, pg, ix: (b, g, 0, 0, 0)))
    return pl.pallas_call(
        functools.partial(_sel_sample_kernel, ts=ts, past=past, cols=cols),
        grid_spec=grid_spec,
        out_shape=jax.ShapeDtypeStruct((ns, NSA_KV_HEADS, ts, rows, D_HEAD), F32),
        compiler_params=_params("arbitrary", "arbitrary"),
        name="sel_sample",
    )(pages.reshape(-1), idx.reshape(-1), slopes, q, kv_new, *([cache] * (ts * TOP_N)))


def _win_sample_kernel(slopes_ref, q_ref, buf_ref, new_ref, o_ref, *, ts, past):
    rows = NSA_HEADS * ts
    w = buf_ref.shape[2]
    rr = lax.broadcasted_iota(jnp.int32, (rows, 1), 0)
    slope = jnp.zeros((rows, 1), F32)
    for h in range(NSA_HEADS):
        slope = jnp.where(rr // ts == h, slopes_ref[h], slope)
    pos = past + lax.rem(rr, ts)
    n_new = new_ref.shape[1]
    kb_pos = past - w + lax.broadcasted_iota(jnp.int32, (1, w), 1)
    kn_pos = past + lax.broadcasted_iota(jnp.int32, (1, n_new), 1)
    d_b = pos - kb_pos
    d_n = pos - kn_pos
    ok_b = (d_b >= 0) & (d_b < WINDOW) & (kb_pos >= 0)
    ok_n = (d_n >= 0) & (d_n < WINDOW)
    bias_b = slope * d_b.astype(F32)
    bias_n = slope * d_n.astype(F32)
    for u in range(q_ref.shape[0]):
        q = q_ref[u]
        new = new_ref[u]
        s_b = _mm(q, buf_ref[u, :LANES, :].astype(BF16)) - bias_b
        s_n = _nt(q, new[:, :LANES].astype(BF16)) - bias_n
        mx = jnp.maximum(jnp.max(jnp.where(ok_b, s_b, NEG), axis=1, keepdims=True),
                         jnp.max(jnp.where(ok_n, s_n, NEG), axis=1, keepdims=True))
        mx = jnp.where(mx > 0.5 * NEG, mx, 0.0)
        e_b = jnp.where(ok_b, jnp.exp(s_b - mx), 0.0)
        e_n = jnp.where(ok_n, jnp.exp(s_n - mx), 0.0)
        den = jnp.maximum(jnp.sum(e_b, axis=1, keepdims=True) + jnp.sum(e_n, axis=1, keepdims=True), 1e-30)
        o_ref[u] = (_nt((e_b / den).astype(BF16), buf_ref[u, LANES:, :].astype(BF16))
                    + _mm((e_n / den).astype(BF16), new[:, LANES:].astype(BF16)))


def _win_sample(slopes, q, win_buf_t, kv_new, ts, past):
    ns, rows, _ = q.shape
    per = math.gcd(ns, 4)
    return pl.pallas_call(
        functools.partial(_win_sample_kernel, ts=ts, past=past),
        grid=(ns // per,),
        in_specs=[pl.BlockSpec(memory_space=pltpu.SMEM),
                  pl.BlockSpec((per, rows, LANES), lambda b: (b, 0, 0)),
                  pl.BlockSpec((per,) + win_buf_t.shape[1:], lambda b: (b, 0, 0)),
                  pl.BlockSpec((per,) + kv_new.shape[1:], lambda b: (b, 0, 0))],
        out_specs=pl.BlockSpec((per, rows, LANES), lambda b: (b, 0, 0)),
        out_shape=jax.ShapeDtypeStruct((ns, rows, LANES), F32),
        compiler_params=_params("arbitrary"),
        name="win_sample",
    )(slopes, q, win_buf_t, kv_new)


def _merge_sample_kernel(cmp_ref, sel_ref, win_ref, gate_ref, o_ref):
    lane = lax.broadcasted_iota(jnp.int32, (1, LANES), 1)
    low = lane < D_HEAD
    for g in range(NSA_KV_HEADS):
        for b in range(NSA_REP // 2):
            col = g * NSA_REP * D_HEAD + b * LANES
            total = None
            for br, ref in enumerate((cmp_ref, sel_ref, win_ref)):
                first, second = ref[g, 2 * b], ref[g, 2 * b + 1]
                if g == 0:
                    val = jnp.where(low, first, pltpu.roll(second, D_HEAD, 1))
                else:
                    val = jnp.where(low, pltpu.roll(first, D_HEAD, 1), second)
                gate = gate_ref[:, br * NSA_WIDTH + col: br * NSA_WIDTH + col + LANES].astype(F32)
                total = gate * val if total is None else total + gate * val
            o_ref[:, col:col + LANES] = total.astype(o_ref.dtype)


def _merge_sample(o_cmp, o_sel, o_win, gates):
    rows = gates.shape[0]
    full = lambda a: pl.BlockSpec(a.shape, lambda i: (0,) * a.ndim)
    return pl.pallas_call(
        _merge_sample_kernel,
        grid=(1,),
        in_specs=[full(o_cmp), full(o_sel), full(o_win), full(gates)],
        out_specs=pl.BlockSpec((rows, NSA_WIDTH), lambda i: (0, 0)),
        out_shape=jax.ShapeDtypeStruct((rows, NSA_WIDTH), BF16),
        compiler_params=_params("arbitrary"),
        name="merge_sample",
    )(o_cmp, o_sel, o_win, gates)


def _dup_kv_cols(w):
    d = w.shape[0]
    return jnp.repeat(w.reshape(d, 2 * NSA_KV_HEADS, 1, D_HEAD), 2, axis=2).reshape(d, 2 * KV_WIDTH)


def _gate_cols(w):
    d = w.shape[0]
    per = w.reshape(d, NSA_HEADS, N_NSA_BRANCHES).transpose(0, 2, 1)
    return jnp.repeat(per[..., None], D_HEAD, axis=3).reshape(d, N_NSA_BRANCHES * NSA_WIDTH)


def _expand_q_cols(w):
    d = w.shape[0]
    place = jax.nn.one_hot(jnp.arange(NSA_HEADS) // NSA_REP, NSA_KV_HEADS, dtype=w.dtype)
    return (w.reshape(d, NSA_HEADS, 1, D_HEAD) * place[None, :, :, None]).reshape(d, NSA_HEADS * LANES)


def _pad_rows(a, rows):
    return jnp.pad(a, [(0, 0)] * (a.ndim - 2) + [(0, rows - a.shape[-2]), (0, 0)])


def _keys_on_lanes(a):
    b, s = a.shape[:2]
    return jnp.transpose(a, (0, 2, 3, 4, 1)).reshape(b, -1, s)


def _groups(spec):
    groups, c = [], 0
    for width, act, dts in spec:
        groups.append((c, width, act, dts))
        c += width
    return tuple(groups)


def _prompt_layer(x, mods, lw, slopes, n, t, final):
    (w_in, g_mix, g_mlp, g_final, cmp_w, w_br_sb, w_br_nsa, w_out, w_up, w_down) = lw
    sh1, sc1, gt1, sh2, sc2, gt2 = mods
    tm = min(256, t)
    w_cat = jnp.concatenate([
        w_in[:, _C_SBQ:_C_SBKV] * ATTN_SCALE, w_in[:, _C_SBKV:_C_NSQ], w_in[:, _C_NSQ:_C_CMP] * ATTN_SCALE,
        w_in[:, _C_CMP:_C_SEL], _dup_kv_cols(w_in[:, _C_SEL:_C_WIN]), _dup_kv_cols(w_in[:, _C_WIN:_C_GATE]),
        _gate_cols(w_in[:, _C_GATE:_C_GSB])], axis=1).astype(BF16)
    groups = _groups(((SB_WIDTH, None, (BF16,)), (2 * SB_WIDTH, None, (BF16,)), (NSA_WIDTH, None, (BF16,)),
                      (KV_WIDTH, None, (F32,)), (2 * KV_WIDTH, None, (BF16,)), (2 * KV_WIDTH, None, (BF16,)),
                      (N_NSA_BRANCHES * NSA_WIDTH, "sigmoid", (BF16,))))
    w_t = jnp.concatenate([w_in[:, _C_SBKV:_C_NSQ], w_in[:, _C_CMP:_C_GATE]], axis=1).T.astype(BF16)
    t_groups = ((0, 2 * SB_WIDTH), (2 * SB_WIDTH, KV_WIDTH), (2 * SB_WIDTH + KV_WIDTH, KV_WIDTH),
                (2 * SB_WIDTH + 2 * KV_WIDTH, KV_WIDTH))
    (sbq, sbkv_b, nsq, cmp_kv, sel_dup, win_dup, gates, sbkv_t, cmp_t, sel_t, win_t) = _project(
        x, g_mix, sc1, sh1, w_cat, groups, tm, t, w_t, t_groups)
    rows_first = lambda a, h: a.reshape(n, 2, h, D_HEAD, a.shape[-1]).transpose(0, 4, 1, 2, 3)
    states = (rows_first(sbkv_t, SB_HEADS), rows_first(cmp_t, NSA_KV_HEADS), rows_first(sel_t, NSA_KV_HEADS),
              rows_first(win_t[:, :, t - min(WINDOW, t):], NSA_KV_HEADS))
    o_sb = _sb_prompt(sbq, sbkv_b, n, t, min(256, t))
    cw = jnp.repeat(cmp_w.reshape(CMP_BLOCK, 2 * NSA_KV_HEADS), D_HEAD, axis=1)
    o_nsa = _nsa_prompt(slopes, nsq, cmp_kv, cw, sel_dup, win_dup, gates, n, t, LANES)
    x1 = _mix(x, g_mix, sc1, sh1, gt1, o_sb, o_nsa, w_in[:, _C_GSB:].astype(BF16), w_br_sb.astype(BF16),
              w_br_nsa.astype(BF16), w_out.astype(BF16), tm, t)
    x2 = _mlp(x1, g_mlp, sc2, sh2, gt2, g_final, w_up.astype(BF16), w_down.astype(BF16), tm, t, final)
    return x2, states


def _sample_mixers(sbq, sbkv_b, nsqx, sel_kv, win_kv, gates, cmp_w, slopes, caches, page_table, ns, ts):
    cache_sb, cache_cmp, cache_sel, win_buf = caches
    m = ns * ts
    n_pages = page_table.shape[1]
    past = n_pages * PAGE_SIZE
    g_, r_ = NSA_KV_HEADS, NSA_REP
    q_rep = jnp.broadcast_to(sbq.reshape(ns, ts, 1, SB_WIDTH), (ns, ts, SB_HEADS, SB_WIDTH))
    q_rep = q_rep.reshape(ns, ts * SB_HEADS, SB_WIDTH)
    kv_new = _pad_rows(sbkv_b.reshape(ns, ts, 2 * SB_WIDTH), PAGE_SIZE)
    o_sb = _sb_sample(page_table, q_rep, kv_new, _keys_on_lanes(cache_sb), math.gcd(n_pages, 32)).reshape(m, SB_WIDTH)

    qx = nsqx.reshape(ns, ts, g_, r_, LANES)
    q_cmp = _pad_rows(qx.transpose(0, 3, 2, 1, 4).reshape(ns, r_, g_ * ts, LANES), 16)
    q_sel = jnp.stack([qx[:, :, g, :, g * D_HEAD:(g + 1) * D_HEAD] for g in range(g_)], axis=1)
    q_sel = _pad_rows(q_sel, 16)
    q_win = qx.transpose(0, 2, 3, 1, 4).reshape(ns, NSA_HEADS * ts, LANES)
    cw_t = jnp.tile(jnp.repeat(cmp_w.reshape(CMP_BLOCK, 2 * g_).T, D_HEAD, axis=0), (1, PAGE_SIZE // CMP_BLOCK))
    o_cmp, idx = _cmp_sample(page_table, slopes, q_cmp, cw_t, _keys_on_lanes(cache_cmp), math.gcd(n_pages, 32), ts)
    kv_rows = lambda a: _pad_rows(a.reshape(ns, ts, KV_WIDTH), 16)
    sel_new = _pad_rows(sel_kv.reshape(ns, ts, 2, g_, D_HEAD).transpose(0, 3, 2, 1, 4), 16)
    sel_slabs = _keys_on_lanes(cache_sel).reshape(-1, 2, g_, D_HEAD, PAGE_SIZE)
    o_sel = _sel_sample(page_table, idx[:, :, :g_ * ts], slopes, q_sel, sel_new, sel_slabs, ts)
    o_win = _win_sample(slopes, q_win, _keys_on_lanes(win_buf), kv_rows(win_kv), ts, past)
    oc = o_cmp[:, :, :g_ * ts].reshape(ns, r_, g_, ts, LANES).transpose(2, 1, 0, 3, 4).reshape(g_, r_, m, LANES)
    os_ = o_sel[:, :, :, :r_].transpose(1, 3, 0, 2, 4).reshape(g_, r_, m, D_HEAD)
    os_ = jnp.stack([jnp.pad(os_[g], ((0, 0), (0, 0), (g * D_HEAD, (g_ - 1 - g) * D_HEAD))) for g in range(g_)])
    ow = o_win.reshape(ns, g_, r_, ts, LANES).transpose(1, 2, 0, 3, 4).reshape(g_, r_, m, LANES)
    return o_sb, _merge_sample(oc, os_, ow, gates)


def _sample_layer(x, mods, lw, slopes, caches, page_table, ns, ts, final):
    (w_in, g_mix, g_mlp, g_final, cmp_w, w_br_sb, w_br_nsa, w_out, w_up, w_down) = lw
    sh1, sc1, gt1, sh2, sc2, gt2 = mods
    tm = sc1[0].shape[1]
    w_cat = jnp.concatenate([
        w_in[:, _C_SBQ:_C_SBKV] * ATTN_SCALE, w_in[:, _C_SBKV:_C_NSQ],
        _expand_q_cols(w_in[:, _C_NSQ:_C_CMP] * ATTN_SCALE), w_in[:, _C_CMP:_C_GATE],
        _gate_cols(w_in[:, _C_GATE:_C_GSB])], axis=1).astype(BF16)
    groups = _groups(((SB_WIDTH, None, (BF16,)), (2 * SB_WIDTH, None, (F32, BF16)), (NSA_HEADS * LANES, None, (BF16,)),
                      (KV_WIDTH, None, (F32,)), (KV_WIDTH, None, (F32,)), (KV_WIDTH, None, (F32,)),
                      (N_NSA_BRANCHES * NSA_WIDTH, "sigmoid", (BF16,))))
    sbq, sbkv, sbkv_b, nsqx, cmp_kv, sel_kv, win_kv, gates = _project(x, g_mix, sc1, sh1, w_cat, groups, tm, ts)
    o_sb, o_nsa = _sample_mixers(sbq, sbkv_b, nsqx, sel_kv, win_kv, gates, cmp_w, slopes, caches, page_table, ns, ts)
    x1 = _mix(x, g_mix, sc1, sh1, gt1, o_sb, o_nsa, w_in[:, _C_GSB:].astype(BF16), w_br_sb.astype(BF16),
              w_br_nsa.astype(BF16), w_out.astype(BF16), tm, ts)
    x2 = _mlp(x1, g_mlp, sc2, sh2, gt2, g_final, w_up.astype(BF16), w_down.astype(BF16), tm, ts, final)
    return x2, (sbkv, cmp_kv, sel_kv, win_kv)


def kernel(x_prompt, x_sample, cache_sb_kv, cache_cmp_kv, cache_sel_kv, state_win_kv, page_table, c_prompt, c_sample,
           w_ada, b_ada, g_mix, g_mlp, g_final, w_in, cmp_w, w_br_sb, w_br_nsa, w_out, w_up, w_down):
    depth = w_in.shape[0]
    n, t, d = x_prompt.shape
    ns, ts, _ = x_sample.shape
    slopes = jnp.exp2(-8.0 * (jnp.arange(NSA_HEADS, dtype=F32) + 1.0) / NSA_HEADS)
    xp = x_prompt.reshape(n * t, d)
    xs = x_sample.reshape(ns * ts, d)
    ms = ns * ts
    tms = min(LANES, ms)
    c_all = jnp.concatenate([c_prompt, c_sample], axis=0)
    st_p = ([], [], [], [])
    st_s = ([], [], [], [])
    for l in range(depth):
        final = l == depth - 1
        lw = (w_in[l], g_mix[l].reshape(1, d), g_mlp[l].reshape(1, d), g_final.reshape(1, d), cmp_w[l],
              w_br_sb[l], w_br_nsa[l], w_out[l], w_up[l], w_down[l])
        mod = _adaln(c_all, w_ada[l], b_ada[l])
        mod_p = mod[:n].reshape(n, 1, 6 * d)
        mod_s = jnp.repeat(mod[n:], ts, axis=0).reshape(ms // tms, tms, 6 * d)
        mods_p = [(mod_p, k, d) for k in range(6)]
        mods_s = [(mod_s, k, d) for k in range(6)]
        xp, sp = _prompt_layer(xp, mods_p, lw, slopes, n, t, final)
        caches = (cache_sb_kv[l], cache_cmp_kv[l], cache_sel_kv[l], state_win_kv[l])
        xs, ss = _sample_layer(xs, mods_s, lw, slopes, caches, page_table, ns, ts, final)
        kv_s = (ns, ts, 2, NSA_KV_HEADS, D_HEAD)
        win_new = jnp.concatenate([state_win_kv[l], ss[3].reshape(kv_s)], axis=1)
        ss = (ss[0].reshape(ns, ts, 2, SB_HEADS, D_HEAD), ss[1].reshape(kv_s), ss[2].reshape(kv_s),
              win_new[:, win_new.shape[1] - min(WINDOW, win_new.shape[1]):])
        for acc, s in zip(st_p, sp):
            acc.append(s)
        for acc, s in zip(st_s, ss):
            acc.append(s)
    return (xp.reshape(n, t, d), xs.reshape(ns, ts, d), jnp.stack(st_p[0]), jnp.stack(st_s[0]),
            jnp.stack(st_p[1]), jnp.stack(st_s[1]), jnp.stack(st_p[2]), jnp.stack(st_s[2]),
            jnp.stack(st_p[3]), jnp.stack(st_s[3]))
```
